```python
import math
import jax, jax.numpy as jnp
from jax import lax
import numpy as np

D_MODEL = 1024
BATCH = 4
SEQ = 4096
DEPTH = 4
DEC_BATCH = 128
DEC_SEQ = 1
PAST_LEN = 2048
PAGE_SIZE = 128

D_MIX = D_MODEL
H_A = 4
DK_A = D_MODEL // 16
DV_A = 2 * DK_A
W_A = H_A * DV_A
QK_A = H_A * 2 * DK_A
G_B = 4
POOL_WINDOWS = (2, 4, 8, 16)
W_B = D_MIX // 4
C_B = W_B // G_B
POOL_BUF = max(POOL_WINDOWS) - 1
H_C = 4
W_C = D_MIX - W_A - W_B
DK_C = W_C // H_C
DV_C = W_C // H_C
D_IN = 2 * QK_A + W_A + W_B + 3 * W_C + D_MIX
SPLITS = (QK_A, 2 * QK_A, 2 * QK_A + W_A, 2 * QK_A + W_A + W_B,
          2 * QK_A + W_A + W_B + W_C, 2 * QK_A + W_A + W_B + 2 * W_C,
          2 * QK_A + W_A + W_B + 3 * W_C)
Q_BLOCK = 128
HGRN_CHUNK = 64
EPS = 1e-6
MASK_VALUE = -1e30

kernel_name = 'hybrid_diffattn_pool_hgrn2_step'


def _rmsnorm(x, g):
    x32 = x.astype(jnp.float32)
    y = x32 * lax.rsqrt(jnp.mean(x32 * x32, axis=-1, keepdims=True) + EPS)
    return (y * g.astype(jnp.float32)).astype(x.dtype)


def _alibi_slopes():
    return jnp.asarray([2.0 ** (-8.0 * (h + 1) / H_A) for h in range(H_A)], dtype=jnp.float32)


def _diff_attn(q, k, v, pos_q, pos_k, lam):
    s = jnp.einsum('bqhcd,bkhcd->bhcqk', q, k, preferred_element_type=jnp.float32) * (DK_A ** -0.5)
    dist = (pos_q[:, None] - pos_k[None, :]).astype(jnp.float32)
    s = s - _alibi_slopes()[None, :, None, None, None] * dist
    s = jnp.where(dist >= 0, s, MASK_VALUE)
    p = jax.nn.softmax(s, axis=-1)
    a = p[:, :, 0] - lam * p[:, :, 1]
    return jnp.einsum('bhqk,bkhe->bqhe', a, v.astype(jnp.float32))


def _attend_prompt(q, k, v, lam):
    B, S = q.shape[:2]
    nb = S // Q_BLOCK
    qb = q.reshape(B, nb, Q_BLOCK, H_A, 2, DK_A).swapaxes(0, 1)
    pos_k = jnp.arange(S)

    def block(args):
        qi, i = args
        pos_q = i * Q_BLOCK + jnp.arange(Q_BLOCK)
        return _diff_attn(qi, k, v, pos_q, pos_k, lam)

    o = lax.map(block, (qb, jnp.arange(nb)))
    return o.swapaxes(0, 1).reshape(B, S, H_A, DV_A)


def _attend_paged(q, k, v, lam, cache_k, cache_v, page_table, l):
    DB, L = q.shape[:2]
    past = page_table.shape[1] * PAGE_SIZE
    kp = cache_k[l, page_table].reshape(DB, past, H_A, 2, DK_A).astype(k.dtype)
    vp = cache_v[l, page_table].reshape(DB, past, H_A, DV_A).astype(v.dtype)
    k_all = jnp.concatenate([kp, k], axis=1)
    v_all = jnp.concatenate([vp, v], axis=1)
    pos_k = jnp.arange(past + L)
    pos_q = past + jnp.arange(L)
    return _diff_attn(q, k_all, v_all, pos_q, pos_k, lam)


def _pool_mix(prev, u, pos0, w, scale):
    B, L, _ = u.shape
    ext = jnp.concatenate([prev.astype(u.dtype), u], axis=1)
    cs = jnp.concatenate([jnp.zeros((B, 1, W_B), jnp.float32),
                          jnp.cumsum(ext.astype(jnp.float32), axis=1)], axis=1)
    end = cs[:, POOL_BUF + 1:]
    pos = pos0 + jnp.arange(L)
    outs = []
    for gi, win in enumerate(POOL_WINDOWS):
        sl = slice(gi * C_B, (gi + 1) * C_B)
        start = cs[:, POOL_BUF + 1 - win:POOL_BUF + 1 - win + L, sl]
        cnt = jnp.minimum(pos + 1, win).astype(jnp.float32)[None, :, None]
        outs.append((end[..., sl] - start) / cnt)
    pooled = jnp.concatenate(outs, axis=-1) - u.astype(jnp.float32)
    mixed = jnp.einsum('blgc,gcd->blgd', pooled.reshape(B, L, G_B, C_B), w.astype(jnp.float32))
    out = mixed.reshape(B, L, W_B) * scale.astype(jnp.float32)
    return out, ext[:, -POOL_BUF:]


def _hgrn_chunk(S, q, k, v, g):
    C = q.shape[1]
    G = jnp.cumsum(g, axis=1)
    inter = jnp.einsum('bthd,bhde->bthe', q * jnp.exp(G), S)
    causal = jnp.tril(jnp.ones((C, C), dtype=bool))[None, :, :, None, None]
    diff = G[:, :, None] - G[:, None, :]
    decay = jnp.where(causal, jnp.exp(jnp.minimum(diff, 0.0)), 0.0)
    A = jnp.sum(q[:, :, None] * k[:, None, :] * decay, axis=-1)
    intra = jnp.einsum('btsh,bshe->bthe', A, v)
    G_last = G[:, -1]
    S_new = jnp.exp(G_last)[..., None] * S + jnp.einsum(
        'bshd,bshe->bhde', k * jnp.exp(G_last[:, None] - G), v)
    return S_new, inter + intra


def _hgrn_scan(S0, q, k, v, g, chunk):
    B, L = q.shape[:2]
    n = L // chunk

    def split(a):
        return a.reshape((B, n, chunk) + a.shape[2:]).swapaxes(0, 1)

    def step(S, xs):
        return _hgrn_chunk(S, *xs)

    S_fin, o = lax.scan(step, S0, (split(q), split(k), split(v), split(g)))
    return S_fin, o.swapaxes(0, 1).reshape(B, L, H_C, DV_C)


def _hgrn_branch(fc, ic, qc, lb, S0, chunk, gain):
    B, L, _ = fc.shape
    xf = fc.astype(jnp.float32).reshape(B, L, H_C, DK_C)
    lb = lb.reshape(H_C, DK_C)
    sig = jax.nn.sigmoid(xf)
    f = lb + (1.0 - lb) * sig
    g = jnp.log(f)
    k = (1.0 - lb) * (1.0 - sig)
    v = ic.astype(jnp.float32).reshape(B, L, H_C, DV_C)
    q = jax.nn.silu(qc.astype(jnp.float32)).reshape(B, L, H_C, DK_C)
    S_fin, o = _hgrn_scan(S0.astype(jnp.float32), q, k, v, g, chunk)
    return _rmsnorm(o, gain).reshape(B, L, W_C), S_fin


def _trunk(x, pos0, attend, pool_prev, S0, chunk, params):
    (ln_gain, w_in, qn_gain, kn_gain, lam_q1, lam_k1, lam_q2, lam_k2, subln_gain,
     pool_w, pool_scale, hgrn_lb, hgrn_norm_gain, w_out) = params
    B, L, _ = x.shape
    dt = x.dtype
    sm = jax.nn.softmax(hgrn_lb.astype(jnp.float32), axis=0)
    lb_all = jnp.cumsum(sm, axis=0) - sm[0]
    ks, vs, pools, states = [], [], [], []
    for l in range(DEPTH):
        h = _rmsnorm(x, ln_gain[l])
        z = jnp.einsum('bld,de->ble', h, w_in[l])
        qa, ka, va, ub, fc, ic, qc, gate = jnp.split(z, SPLITS, axis=-1)
        lam_init = 0.8 - 0.6 * math.exp(-0.3 * l)
        lam = (jnp.exp(jnp.sum(lam_q1[l].astype(jnp.float32) * lam_k1[l].astype(jnp.float32)))
               - jnp.exp(jnp.sum(lam_q2[l].astype(jnp.float32) * lam_k2[l].astype(jnp.float32)))
               + lam_init)
        qa = _rmsnorm(qa.reshape(B, L, H_A, 2, DK_A), qn_gain[l])
        ka = _rmsnorm(ka.reshape(B, L, H_A, 2, DK_A), kn_gain[l])
        va = va.reshape(B, L, H_A, DV_A)
        oa = attend(l, qa, ka, va, lam)
        oa = _rmsnorm(oa, subln_gain[l]) * (1.0 - lam_init)
        ob, pool_new = _pool_mix(pool_prev[l], ub, pos0, pool_w[l], pool_scale[l])
        oc, S_new = _hgrn_branch(fc, ic, qc, lb_all[l], S0[l], chunk, hgrn_norm_gain[l])
        mix = jnp.concatenate([oa.reshape(B, L, W_A).astype(dt), ob.astype(dt), oc.astype(dt)],
                              axis=-1) * jax.nn.silu(gate)
        x = x + jnp.einsum('ble,ed->bld', mix, w_out[l])
        ks.append(ka.reshape(B, L, H_A, 2 * DK_A))
        vs.append(va)
        pools.append(pool_new)
        states.append(S_new)
    return x, jnp.stack(ks), jnp.stack(vs), jnp.stack(pools), jnp.stack(states)


def setup_inputs(seed: int = 0) -> dict:
    key = jax.random.key(seed)
    kk = jax.random.split(key, 24)
    n_pages = PAST_LEN // PAGE_SIZE
    n_used = DEC_BATCH * n_pages
    n_phys = n_used + n_used // 4
    f32 = jnp.float32

    def nrm(k, shape, s=1.0):
        return jax.random.normal(k, shape, f32) * s

    page_table = jax.random.permutation(kk[6], n_phys)[:n_used].reshape(DEC_BATCH, n_pages).astype(jnp.int32)
    return {
        'x_prompt': nrm(kk[0], (BATCH, SEQ, D_MODEL)),
        'x_sample': nrm(kk[1], (DEC_BATCH, DEC_SEQ, D_MODEL)),
        'cache_k': nrm(kk[2], (DEPTH, n_phys, PAGE_SIZE, H_A, 2 * DK_A)),
        'cache_v': nrm(kk[3], (DEPTH, n_phys, PAGE_SIZE, H_A, DV_A)),
        'state_pool': nrm(kk[4], (DEPTH, DEC_BATCH, POOL_BUF, W_B)),
        'state_hgrn': nrm(kk[5], (DEPTH, DEC_BATCH, H_C, DK_C, DV_C), 0.5),
        'page_table': page_table,
        'ln_gain': 1.0 + nrm(kk[7], (DEPTH, D_MODEL), 0.02),
        'w_in': nrm(kk[8], (DEPTH, D_MODEL, D_IN), D_MODEL ** -0.5),
        'qn_gain': 1.0 + nrm(kk[9], (DEPTH, DK_A), 0.02),
        'kn_gain': 1.0 + nrm(kk[10], (DEPTH, DK_A), 0.02),
        'lam_q1': nrm(kk[11], (DEPTH, DK_A), 0.1),
        'lam_k1': nrm(kk[12], (DEPTH, DK_A), 0.1),
        'lam_q2': nrm(kk[13], (DEPTH, DK_A), 0.1),
        'lam_k2': nrm(kk[14], (DEPTH, DK_A), 0.1),
        'subln_gain': 1.0 + nrm(kk[15], (DEPTH, DV_A), 0.02),
        'pool_w': nrm(kk[16], (DEPTH, G_B, C_B, C_B), C_B ** -0.5),
        'pool_scale': 1.0 + nrm(kk[17], (DEPTH, W_B), 0.02),
        'hgrn_lb': nrm(kk[18], (DEPTH, W_C)),
        'hgrn_norm_gain': 1.0 + nrm(kk[19], (DEPTH, DV_C), 0.02),
        'w_out': nrm(kk[20], (DEPTH, D_MIX, D_MODEL), D_MIX ** -0.5),
    }


def reference(x_prompt, x_sample, cache_k, cache_v, state_pool, state_hgrn, page_table,
              ln_gain, w_in, qn_gain, kn_gain, lam_q1, lam_k1, lam_q2, lam_k2, subln_gain,
              pool_w, pool_scale, hgrn_lb, hgrn_norm_gain, w_out):
    params = (ln_gain, w_in, qn_gain, kn_gain, lam_q1, lam_k1, lam_q2, lam_k2, subln_gain,
              pool_w, pool_scale, hgrn_lb, hgrn_norm_gain, w_out)
    B = x_prompt.shape[0]
    pool0 = jnp.zeros((DEPTH, B, POOL_BUF, W_B), x_prompt.dtype)
    S0 = jnp.zeros((DEPTH, B, H_C, DK_C, DV_C), jnp.float32)
    y_prompt, k_prompt, v_prompt, pool_prompt, hgrn_prompt = _trunk(
        x_prompt, 0, lambda l, q, k, v, lam: _attend_prompt(q, k, v, lam),
        pool0, S0, HGRN_CHUNK, params)
    past = page_table.shape[1] * PAGE_SIZE
    y_sample, k_sample, v_sample, pool_sample, hgrn_sample = _trunk(
        x_sample, past,
        lambda l, q, k, v, lam: _attend_paged(q, k, v, lam, cache_k, cache_v, page_table, l),
        state_pool, state_hgrn, 1, params)
    return (y_prompt, y_sample, k_prompt, v_prompt, k_sample, v_sample,
            pool_prompt, pool_sample, hgrn_prompt, hgrn_sample)
```

```python
import functools
import math

import numpy as np
import jax
import jax.numpy as jnp
from jax import lax
from jax.experimental import pallas as pl
from jax.experimental.pallas import tpu as pltpu

F32 = jnp.float32
BF16 = jnp.bfloat16

D_MODEL = 1024
DEPTH = 4
PAGE_SIZE = 128
H_A = 4
DK_A = 64
DV_A = 128
W_A = H_A * DV_A
QK_A = H_A * 2 * DK_A
G_B = 4
POOL_WINDOWS = (2, 4, 8, 16)
W_B = 256
C_B = W_B // G_B
POOL_BUF = max(POOL_WINDOWS) - 1
H_C = 4
W_C = 256
DK_C = 64
DV_C = 64
D_IN = 2 * QK_A + W_A + W_B + 3 * W_C + D_MODEL
HGRN_CHUNK = 64
HGRN_SUB = 16
EPS = 1e-6
NEG = -1e30

C_Q, C_K, C_V, C_U = 0, QK_A, 2 * QK_A, 2 * QK_A + W_A
C_F = C_U + W_B
C_I = C_F + W_C
C_QC = C_I + W_C
C_GATE = C_QC + W_C

V7X_VMEM_BYTES = 64 * 1024 * 1024
VMEM_LIMIT = 56 * 1024 * 1024


def _dot(a, b):
    return jnp.dot(a, b, preferred_element_type=F32)


def _dot_nt(a, b):
    return lax.dot_general(a, b, (((1,), (1,)), ((), ())), preferred_element_type=F32)


def _dot_tn(a, b):
    return lax.dot_general(a, b, (((0,), (0,)), ((), ())), preferred_element_type=F32)


def _split2(x):
    hi = x.astype(BF16)
    lo = (x - hi.astype(F32)).astype(BF16)
    return hi, lo


def _split3(x):
    hi = x.astype(BF16)
    r = x - hi.astype(F32)
    mid = r.astype(BF16)
    lo = (r - mid.astype(F32)).astype(BF16)
    return hi, mid, lo


def _group_sum(x, ones_bd):
    hi, lo = _split2(x)
    return _dot(hi, ones_bd) + _dot(lo, ones_bd)


def _sigmoid(x):
    return 1.0 / (1.0 + jnp.exp(-x))


def _inproj_body(x_ref, lng_ref, w_ref, qg_ref, kg_ref, lb_ref, bd_ref):
    x = x_ref[...]
    h = x * lax.rsqrt(jnp.mean(x * x, axis=-1, keepdims=True) + EPS) * lng_ref[...]
    hb = h.astype(BF16)
    bd = bd_ref[...]

    def proj(lo, hi):
        return _dot(hb, w_ref[:, lo:hi])

    def head_norm(z, gain):
        ss = _group_sum(z * z, bd)
        return z * lax.rsqrt(ss * (1.0 / DK_A) + EPS) * gain

    qn = head_norm(proj(C_Q, C_K), qg_ref[...]) * (DK_A ** -0.5)
    kn = head_norm(proj(C_K, C_V), kg_ref[...])
    v = proj(C_V, C_U)
    u = proj(C_U, C_F)
    lb = lb_ref[...]
    sig = _sigmoid(proj(C_F, C_I))
    g = jnp.log(lb + (1.0 - lb) * sig)
    kk = (1.0 - lb) * (1.0 - sig)
    vc = proj(C_I, C_QC)
    qc = proj(C_QC, C_GATE)
    qh = qc * _sigmoid(qc)
    gate = proj(C_GATE, D_IN)
    sg = gate * _sigmoid(gate)
    return qn, kn, v, u, g, kk, vc, qh, sg


def _inproj_prompt_kernel(x_ref, lng_ref, w_ref, qg_ref, kg_ref, lb_ref, bd_ref,
                          q_ref, k_ref, kb_ref, v_ref, vb_ref, u_ref, g_ref, kk_ref, vc_ref,
                          qh_ref, sg_ref):
    qn, kn, v, u, g, kk, vc, qh, sg = _inproj_body(x_ref, lng_ref, w_ref, qg_ref, kg_ref, lb_ref, bd_ref)
    q_ref[...] = qn.astype(BF16)
    k_ref[...] = kn
    kb_ref[...] = kn.astype(BF16)
    v_ref[...] = v
    vb_ref[...] = v.astype(BF16)
    u_ref[...] = u
    g_ref[...] = g
    kk_ref[...] = kk
    vc_ref[...] = vc
    qh_ref[...] = qh
    sg_ref[...] = sg


def _inproj_decode_kernel(x_ref, lng_ref, w_ref, qg_ref, kg_ref, lb_ref, bd_ref,
                          q_ref, k_ref, v_ref, u_ref, gt_ref, kkt_ref, vc_ref, qht_ref, sg_ref):
    qn, kn, v, u, g, kk, vc, qh, sg = _inproj_body(x_ref, lng_ref, w_ref, qg_ref, kg_ref, lb_ref, bd_ref)
    q_ref[...] = qn
    k_ref[...] = kn
    v_ref[...] = v
    u_ref[...] = u
    gt_ref[...] = g.T
    kkt_ref[...] = kk.T
    vc_ref[...] = vc
    qht_ref[...] = qh.T
    sg_ref[...] = sg


def _inproj_in_specs(tm):
    row = lambda i: (i, 0)
    fix = lambda i: (0, 0)
    return [
        pl.BlockSpec((tm, D_MODEL), row),
        pl.BlockSpec((1, D_MODEL), fix),
        pl.BlockSpec((D_MODEL, D_IN), fix),
        pl.BlockSpec((1, QK_A), fix),
        pl.BlockSpec((1, QK_A), fix),
        pl.BlockSpec((1, W_C), fix),
        pl.BlockSpec((QK_A, QK_A), fix),
    ]


def _inproj_prompt(x, lng, w, qg, kg, lb, bd, tm):
    t = x.shape[0]
    row = lambda i: (i, 0)
    widths = [(QK_A, BF16), (QK_A, F32), (QK_A, BF16), (W_A, F32), (W_A, BF16), (W_B, F32),
              (W_C, F32), (W_C, F32), (W_C, F32), (W_C, F32), (D_MODEL, F32)]
    return pl.pallas_call(
        _inproj_prompt_kernel,
        grid=(t // tm,),
        in_specs=_inproj_in_specs(tm),
        out_specs=[pl.BlockSpec((tm, wd), row) for wd, _ in widths],
        out_shape=[jax.ShapeDtypeStruct((t, wd), dt) for wd, dt in widths],
        name="inproj_prompt",
        compiler_params=pltpu.CompilerParams(
            dimension_semantics=("parallel",), vmem_limit_bytes=VMEM_LIMIT),
    )(x, lng, w, qg, kg, lb, bd)


def _inproj_decode(x, lng, w, qg, kg, lb, bd):
    t = x.shape[0]
    row = lambda i: (i, 0)
    shapes = [(t, QK_A), (t, QK_A), (t, W_A), (t, W_B), (W_C, t), (W_C, t), (t, W_C), (W_C, t),
              (t, D_MODEL)]
    return pl.pallas_call(
        _inproj_decode_kernel,
        grid=(1,),
        in_specs=_inproj_in_specs(t),
        out_specs=[pl.BlockSpec(s, row) for s in shapes],
        out_shape=[jax.ShapeDtypeStruct(s, F32) for s in shapes],
        name="inproj_decode",
        compiler_params=pltpu.CompilerParams(
            dimension_semantics=("arbitrary",), vmem_limit_bytes=VMEM_LIMIT),
    )(x, lng, w, qg, kg, lb, bd)


def _attn_kernel(lam_ref, slope_ref, q_ref, k_ref, v_ref, sub_ref, o_ref, m_sc, l_sc, acc_sc,
                 *, blk, post_scale):
    hd = pl.program_id(1)
    qi = pl.program_id(2)
    slope = slope_ref[hd]
    lam = lam_ref[0]
    q = q_ref[...]
    lane = lax.broadcasted_iota(jnp.int32, q.shape, 1)
    zero = jnp.zeros_like(q)
    qm = (jnp.where(lane < DK_A, q, zero), jnp.where(lane >= DK_A, q, zero))
    rel = (lax.broadcasted_iota(jnp.int32, (blk, blk), 0)
           - lax.broadcasted_iota(jnp.int32, (blk, blk), 1))
    srel = slope * rel.astype(F32)
    m_sc[...] = jnp.full(m_sc.shape, NEG, F32)
    l_sc[...] = jnp.zeros(l_sc.shape, F32)
    acc_sc[...] = jnp.zeros(acc_sc.shape, F32)

    def step(kj, masked):
        start = pl.multiple_of(kj * blk, blk)
        kb = k_ref[pl.ds(start, blk), :]
        vb = v_ref[pl.ds(start, blk), :]
        off = slope * ((qi - kj) * blk).astype(F32)
        for c in range(2):
            s = _dot_nt(qm[c], kb) - srel - off
            if masked:
                s = jnp.where(rel >= 0, s, NEG)
            m_old = m_sc[c]
            m_new = jnp.maximum(m_old, jnp.max(s, axis=-1, keepdims=True))
            alpha = jnp.exp(m_old - m_new)
            p = jnp.exp(s - m_new)
            l_sc[c] = alpha * l_sc[c] + jnp.sum(p, axis=-1, keepdims=True)
            acc_sc[c] = alpha * acc_sc[c] + _dot(p.astype(BF16), vb)
            m_sc[c] = m_new

    def body(kj, carry):
        step(kj, False)
        return carry

    lax.fori_loop(0, qi, body, 0)
    step(qi, True)
    o = acc_sc[0] / l_sc[0] - lam * (acc_sc[1] / l_sc[1])
    on = o * lax.rsqrt(jnp.mean(o * o, axis=-1, keepdims=True) + EPS) * sub_ref[...]
    o_ref[...] = on * post_scale


def _attn_prompt(lam, slopes, q, k, v, sub, nb, seq, blk, post_scale):
    t = nb * seq
    nq = seq // blk
    smem = pl.BlockSpec(memory_space=pltpu.SMEM)
    return pl.pallas_call(
        functools.partial(_attn_kernel, blk=blk, post_scale=post_scale),
        grid=(nb, H_A, nq),
        in_specs=[
            smem, smem,
            pl.BlockSpec((blk, DV_A), lambda b, h, i: (b * nq + i, h)),
            pl.BlockSpec((seq, DV_A), lambda b, h, i: (b, h)),
            pl.BlockSpec((seq, DV_A), lambda b, h, i: (b, h)),
            pl.BlockSpec((1, DV_A), lambda b, h, i: (0, 0)),
        ],
        out_specs=pl.BlockSpec((blk, DV_A), lambda b, h, i: (b * nq + i, h)),
        out_shape=jax.ShapeDtypeStruct((t, W_A), F32),
        scratch_shapes=[
            pltpu.VMEM((2, blk, 1), F32),
            pltpu.VMEM((2, blk, 1), F32),
            pltpu.VMEM((2, blk, DV_A), F32),
        ],
        name="attn_prompt",
        compiler_params=pltpu.CompilerParams(
            dimension_semantics=("parallel", "parallel", "arbitrary"),
            vmem_limit_bytes=VMEM_LIMIT),
    )(lam, slopes, q, k, v, sub)


def _block_diag_rows(m):
    head = lax.broadcasted_iota(jnp.int32, m.shape, 1) // DK_C
    parts = [jnp.where(head == hp, m, 0.0).astype(BF16) for hp in range(H_C)]
    return jnp.concatenate(parts, axis=0)


def _mix_kernel(x_ref, oa_ref, u_ref, g_ref, kk_ref, vc_ref, qh_ref, sg_ref,
                ltri_ref, hh_ref, pw_ref, ps_ref, hg_ref, wo_ref,
                xo_ref, pool_ref, hst_ref,
                ubuf, kkbuf, gbuf, vcbuf, st_sc, oc_sc, *, tc):
    si = pl.program_id(1)
    ns = pl.num_programs(1)
    pad = HGRN_SUB
    nchunk = tc // HGRN_CHUNK

    @pl.when(si == 0)
    def _():
        st_sc[...] = jnp.zeros(st_sc.shape, F32)
        z = jnp.zeros((pad, W_B), F32)
        ubuf[0:pad, :] = z
        kkbuf[0:pad, :] = z
        gbuf[0:pad, :] = z
        vcbuf[0:pad, :] = z

    u = u_ref[...]
    ubuf[pad:pad + tc, :] = u
    acc = u
    wsum = {}
    for j in range(1, max(POOL_WINDOWS)):
        acc = acc + ubuf[pad - j:pad - j + tc, :]
        if j + 1 in POOL_WINDOWS:
            wsum[j + 1] = acc
    pos = (si * tc + lax.broadcasted_iota(jnp.int32, (tc, 1), 0)).astype(F32)
    lane_b = lax.broadcasted_iota(jnp.int32, (tc, W_B), 1)
    pooled = None
    for gi, win in reversed(list(enumerate(POOL_WINDOWS))):
        term = wsum[win] * (1.0 / jnp.minimum(pos + 1.0, float(win)))
        pooled = term if pooled is None else jnp.where(lane_b < (gi + 1) * C_B, term, pooled)
    pooled = pooled - u
    ob = _dot(pooled.astype(BF16), pw_ref[...]) * ps_ref[...]
    ubuf[0:pad, :] = ubuf[tc:tc + pad, :]

    g = g_ref[...]
    kk = kk_ref[...]
    vc = vc_ref[...]
    qh = qh_ref[...]
    ltri = ltri_ref[...]
    ghi, gmid, glo = _split3(g)
    gc = _dot(ltri, ghi) + _dot(ltri, gmid) + _dot(ltri, glo)
    gc3 = gc.reshape(nchunk, HGRN_CHUNK, W_C)

    def chunk_row(idx):
        r = jnp.broadcast_to(gc3[:, idx:idx + 1, :], (nchunk, HGRN_CHUNK, W_C))
        return r.reshape(tc, W_C)

    rc = lax.broadcasted_iota(jnp.int32, (tc, 1), 0) % HGRN_CHUNK
    eg = jnp.exp(gc)
    qe = qh * eg
    kdec = kk * jnp.exp(chunk_row(HGRN_CHUNK - 1) - gc)
    upper = rc >= 32
    r31 = chunk_row(31)
    qt1 = jnp.where(upper, qh * jnp.exp(jnp.minimum(gc - r31, 0.0)), 0.0)
    kt1 = jnp.where(upper, 0.0, kk * jnp.exp(jnp.minimum(r31 - gc, 0.0)))
    ref2 = jnp.where(upper, chunk_row(47), chunk_row(15))
    odd = ((rc // HGRN_SUB) % 2) == 1
    qt2 = jnp.where(odd, qh * jnp.exp(jnp.minimum(gc - ref2, 0.0)), 0.0)
    kt2 = jnp.where(odd, 0.0, kk * jnp.exp(jnp.minimum(ref2 - gc, 0.0)))
    tb = lax.broadcasted_iota(jnp.int32, (HGRN_CHUNK, H_C * HGRN_CHUNK), 0) // HGRN_SUB
    sb = (lax.broadcasted_iota(jnp.int32, (HGRN_CHUNK, H_C * HGRN_CHUNK), 1) % HGRN_CHUNK) // HGRN_SUB
    mask2 = ((tb % 2) == 1) & (sb == tb - 1)

    kkbuf[pad:pad + tc, :] = kk
    gbuf[pad:pad + tc, :] = gc
    vcbuf[pad:pad + tc, :] = vc
    hh = hh_ref[...]
    r16 = rc % HGRN_SUB
    od = jnp.zeros((tc, W_C), F32)
    for d in range(HGRN_SUB):
        kks = kkbuf[pad - d:pad - d + tc, :]
        gs = gbuf[pad - d:pad - d + tc, :]
        vcs = vcbuf[pad - d:pad - d + tc, :]
        xd = jnp.where(r16 >= d, qh * kks * jnp.exp(jnp.minimum(gc - gs, 0.0)), 0.0)
        od = od + _dot(xd.astype(BF16), hh) * vcs

    bdmask = (lax.broadcasted_iota(jnp.int32, (W_C, W_C), 0) // DK_C
              == lax.broadcasted_iota(jnp.int32, (W_C, W_C), 1) // DK_C)
    for c in range(nchunk):
        sl = slice(c * HGRN_CHUNK, (c + 1) * HGRN_CHUNK)
        st = st_sc[...]
        inter = _dot_nt(qe[sl].astype(BF16), st.astype(BF16))
        a1 = _dot_nt(qt1[sl].astype(BF16), _block_diag_rows(kt1[sl]))
        a2 = _dot_nt(qt2[sl].astype(BF16), _block_diag_rows(kt2[sl]))
        a_off = a1 + jnp.where(mask2, a2, 0.0)
        ooff = _dot(a_off.astype(BF16), _block_diag_rows(vc[sl]))
        oc_sc[sl, :] = inter + ooff + od[sl]
        last = c * HGRN_CHUNK + HGRN_CHUNK - 1
        upd = _dot_tn(vc[sl].astype(BF16), kdec[sl].astype(BF16))
        st_sc[...] = st * eg[last:last + 1, :] + jnp.where(bdmask, upd, 0.0)

    o = oc_sc[...]
    ss = _group_sum(o * o, hh)
    ocn = o * lax.rsqrt(ss * (1.0 / DV_C) + EPS) * hg_ref[...]

    sg = sg_ref[...]
    y = (_dot((oa_ref[...] * sg[:, 0:W_A]).astype(BF16), wo_ref[0:W_A, :])
         + _dot((ob * sg[:, W_A:W_A + W_B]).astype(BF16), wo_ref[W_A:W_A + W_B, :])
         + _dot((ocn * sg[:, W_A + W_B:]).astype(BF16), wo_ref[W_A + W_B:, :]))
    xo_ref[...] = x_ref[...] + y

    @pl.when(si == ns - 1)
    def _():
        pool_ref[0] = u[tc - POOL_BUF:, :]
        hst_ref[0] = st_sc[...]


def _mix_prompt(x, oa, u, g, kk, vc, qh, sg, ltri, hh, pw, ps, hg, wo, nb, seq, tc):
    t = nb * seq
    ns = seq // tc
    row = lambda b, s: (b * ns + s, 0)
    fix = lambda b, s: (0, 0)
    in_specs = [pl.BlockSpec((tc, D_MODEL), row), pl.BlockSpec((tc, W_A), row)]
    in_specs += [pl.BlockSpec((tc, W_C), row) for _ in range(5)]
    in_specs += [
        pl.BlockSpec((tc, D_MODEL), row),
        pl.BlockSpec((tc, tc), fix),
        pl.BlockSpec((W_C, W_C), fix),
        pl.BlockSpec((W_B, W_B), fix),
        pl.BlockSpec((1, W_B), fix),
        pl.BlockSpec((1, W_C), fix),
        pl.BlockSpec((D_MODEL, D_MODEL), fix),
    ]
    return pl.pallas_call(
        functools.partial(_mix_kernel, tc=tc),
        grid=(nb, ns),
        in_specs=in_specs,
        out_specs=[
            pl.BlockSpec((tc, D_MODEL), row),
            pl.BlockSpec((1, POOL_BUF, W_B), lambda b, s: (b, 0, 0)),
            pl.BlockSpec((1, W_C, W_C), lambda b, s: (b, 0, 0)),
        ],
        out_shape=[
            jax.ShapeDtypeStruct((t, D_MODEL), F32),
            jax.ShapeDtypeStruct((nb, POOL_BUF, W_B), F32),
            jax.ShapeDtypeStruct((nb, W_C, W_C), F32),
        ],
        scratch_shapes=[
            pltpu.VMEM((tc + HGRN_SUB, W_B), F32),
            pltpu.VMEM((tc + HGRN_SUB, W_C), F32),
            pltpu.VMEM((tc + HGRN_SUB, W_C), F32),
            pltpu.VMEM((tc + HGRN_SUB, W_C), F32),
            pltpu.VMEM((W_C, W_C), F32),
            pltpu.VMEM((tc, W_C), F32),
        ],
        name="mix_prompt",
        compiler_params=pltpu.CompilerParams(
            dimension_semantics=("parallel", "arbitrary"), vmem_limit_bytes=VMEM_LIMIT),
    )(x, oa, u, g, kk, vc, qh, sg, ltri, hh, pw, ps, hg, wo)


ATT_ROWS = 16


def _paged_attn_kernel(pt_ref, lam_ref, slope_ref, q_ref, kn_ref, vn_ref, sub_ref, *rest,
                       n_pages, post_scale):
    del pt_ref
    k_refs = rest[:n_pages]
    v_refs = rest[n_pages:2 * n_pages]
    o_ref = rest[2 * n_pages]
    s_sc = rest[2 * n_pages + 1]
    rows = PAGE_SIZE * H_A
    past = n_pages * PAGE_SIZE
    lam = lam_ref[0]

    r = lax.broadcasted_iota(jnp.int32, (ATT_ROWS, DV_A), 0)
    lane = lax.broadcasted_iota(jnp.int32, (ATT_ROWS, DV_A), 1)

    def head_rows(row):
        out = jnp.zeros((ATT_ROWS, DV_A), F32)
        for h in range(H_A):
            piece = jnp.broadcast_to(row[:, h * DV_A:(h + 1) * DV_A], (ATT_ROWS, DV_A))
            out = jnp.where(r % H_A == h, piece, out)
        return out

    q = jnp.where(lane // DK_A == r // (ATT_ROWS // 2), head_rows(q_ref[0]), 0.0)
    kn = head_rows(kn_ref[0])
    vn = head_rows(vn_ref[0])
    qb = q.astype(BF16)
    slope = slope_ref[...]

    col = lax.broadcasted_iota(jnp.int32, (ATT_ROWS, rows), 1)
    rr = lax.broadcasted_iota(jnp.int32, (ATT_ROWS, rows), 0)
    valid = (col % H_A) == (rr % H_A)
    tok = col // H_A
    for j in range(n_pages):
        s = _dot_nt(qb, k_refs[j][...].astype(BF16))
        dist = (past - j * PAGE_SIZE - tok).astype(F32)
        s_sc[:, j * rows:(j + 1) * rows] = jnp.where(valid, s - slope * dist, NEG)
    s_self = jnp.sum(q * kn, axis=-1, keepdims=True)
    s_all = s_sc[...]
    m = jnp.maximum(jnp.max(s_all, axis=-1, keepdims=True), s_self)
    p_self = jnp.exp(s_self - m)
    l = p_self
    acc = p_self * vn
    for j in range(n_pages):
        p = jnp.exp(s_all[:, j * rows:(j + 1) * rows] - m)
        l = l + jnp.sum(p, axis=-1, keepdims=True)
        acc = acc + _dot(p.astype(BF16), v_refs[j][...].astype(BF16))
    o16 = acc / l
    half = ATT_ROWS // 2
    o = o16[0:half] - lam * o16[half:ATT_ROWS]
    on = o * lax.rsqrt(jnp.mean(o * o, axis=-1, keepdims=True) + EPS) * sub_ref[...]
    o_ref[0] = (on * post_scale)[0:H_A]


def _attn_paged(pt_flat, lam, slope_rows, q, kn, vn, sub, ck, cv, layer, n_pages, post_scale):
    nb = q.shape[0]
    rows = PAGE_SIZE * H_A
    smem = pl.BlockSpec(memory_space=pltpu.SMEM)
    row3 = lambda b, pt: (b, 0, 0)

    def page_spec(j):
        return pl.BlockSpec((None, None, rows, DV_A),
                            lambda b, pt, j=j: (layer, pt[b * n_pages + j], 0, 0))

    in_specs = [smem,
                pl.BlockSpec((ATT_ROWS, 1), lambda b, pt: (0, 0)),
                pl.BlockSpec((1, 1, QK_A), row3),
                pl.BlockSpec((1, 1, QK_A), row3),
                pl.BlockSpec((1, 1, W_A), row3),
                pl.BlockSpec((1, DV_A), lambda b, pt: (0, 0))]
    in_specs += [page_spec(j) for j in range(n_pages)] * 2
    grid_spec = pltpu.PrefetchScalarGridSpec(
        num_scalar_prefetch=1,
        grid=(nb,),
        in_specs=in_specs,
        out_specs=pl.BlockSpec((1, H_A, DV_A), row3),
        scratch_shapes=[pltpu.VMEM((ATT_ROWS, n_pages * rows), F32)],
    )
    return pl.pallas_call(
        functools.partial(_paged_attn_kernel, n_pages=n_pages, post_scale=post_scale),
        grid_spec=grid_spec,
        out_shape=jax.ShapeDtypeStruct((nb, H_A, DV_A), F32),
        name="attn_paged",
        compiler_params=pltpu.CompilerParams(
            dimension_semantics=("arbitrary",), vmem_limit_bytes=VMEM_LIMIT),
    )(pt_flat, lam, slope_rows, q.reshape(nb, 1, QK_A), kn.reshape(nb, 1, QK_A),
      vn.reshape(nb, 1, W_A), sub, *([ck] * n_pages), *([cv] * n_pages))


def _dec_mix_kernel(x_ref, oa_ref, u_ref, gt_ref, kkt_ref, qht_ref, vc_ref, sg_ref, sp_ref, sh_ref,
                    pw_ref, ps_ref, hg_ref, wo_ref,
                    xo_ref, po_ref, ho_ref, pooled_sc, o_sc, *, nbt, past):
    i = pl.program_id(0)
    ridx = lax.broadcasted_iota(jnp.int32, (POOL_BUF, W_B), 0)
    lane_b = lax.broadcasted_iota(jnp.int32, (1, W_B), 1)
    lane_t = lax.broadcasted_iota(jnp.int32, gt_ref.shape, 1)

    def per_batch(b, carry):
        prev = sp_ref[b]
        urow = u_ref[pl.ds(b, 1), :]
        pooled = None
        for gi, win in reversed(list(enumerate(POOL_WINDOWS))):
            ws = urow + jnp.sum(jnp.where(ridx >= POOL_BUF + 1 - win, prev, 0.0), axis=0, keepdims=True)
            term = ws * (1.0 / min(past + 1, win))
            pooled = term if pooled is None else jnp.where(lane_b < (gi + 1) * C_B, term, pooled)
        pooled_sc[pl.ds(b, 1), :] = pooled - urow
        po_ref[b, 0:POOL_BUF - 1, :] = prev[1:POOL_BUF, :]
        po_ref[b, POOL_BUF - 1:POOL_BUF, :] = urow

        bl = i * nbt + b

        def col(ref):
            return jnp.sum(jnp.where(lane_t == bl, ref[...], 0.0), axis=1, keepdims=True)

        fcol = jnp.exp(col(gt_ref))
        kcol = col(kkt_ref)
        qcol = col(qht_ref)
        vrow = vc_ref[pl.ds(b, 1), :]
        vrows = jnp.concatenate(
            [jnp.broadcast_to(vrow[:, h * DV_C:(h + 1) * DV_C], (DK_C, DV_C)) for h in range(H_C)],
            axis=0)
        s_new = fcol * sh_ref[b] + kcol * vrows
        ho_ref[b] = s_new
        o4 = jnp.sum((s_new * qcol).reshape(H_C, DK_C, DV_C), axis=1)
        for h in range(H_C):
            o_sc[h, pl.ds(b, 1), :] = o4[h:h + 1, :]
        return carry

    lax.fori_loop(0, nbt, per_batch, 0)

    ob = _dot(pooled_sc[...].astype(BF16), pw_ref[...]) * ps_ref[...]
    sg = sg_ref[...]
    y = (_dot((oa_ref[...] * sg[:, 0:W_A]).astype(BF16), wo_ref[0:W_A, :])
         + _dot((ob * sg[:, W_A:W_A + W_B]).astype(BF16), wo_ref[W_A:W_A + W_B, :]))
    for h in range(H_C):
        oh = o_sc[h]
        ohn = oh * lax.rsqrt(jnp.mean(oh * oh, axis=-1, keepdims=True) + EPS) * hg_ref[...]
        lo = W_A + W_B + h * DV_C
        y = y + _dot((ohn * sg[:, lo:lo + DV_C]).astype(BF16), wo_ref[lo:lo + DV_C, :])
    xo_ref[...] = x_ref[...] + y


def _mix_decode(x, oa, u, gt, kkt, qht, vc, sg, sp, sh, pw, ps, hg, wo, nbt, past):
    nb = x.shape[0]
    row = lambda i: (i, 0)
    fix = lambda i: (0, 0)
    row3 = lambda i: (i, 0, 0)
    in_specs = [
        pl.BlockSpec((nbt, D_MODEL), row),
        pl.BlockSpec((nbt, W_A), row),
        pl.BlockSpec((nbt, W_B), row),
        pl.BlockSpec((W_C, nb), fix),
        pl.BlockSpec((W_C, nb), fix),
        pl.BlockSpec((W_C, nb), fix),
        pl.BlockSpec((nbt, W_C), row),
        pl.BlockSpec((nbt, D_MODEL), row),
        pl.BlockSpec((nbt, POOL_BUF, W_B), row3),
        pl.BlockSpec((nbt, W_C, DV_C), row3),
        pl.BlockSpec((W_B, W_B), fix),
        pl.BlockSpec((1, W_B), fix),
        pl.BlockSpec((1, DV_C), fix),
        pl.BlockSpec((D_MODEL, D_MODEL), fix),
    ]
    return pl.pallas_call(
        functools.partial(_dec_mix_kernel, nbt=nbt, past=past),
        grid=(nb // nbt,),
        in_specs=in_specs,
        out_specs=[
            pl.BlockSpec((nbt, D_MODEL), row),
            pl.BlockSpec((nbt, POOL_BUF, W_B), row3),
            pl.BlockSpec((nbt, W_C, DV_C), row3),
        ],
        out_shape=[
            jax.ShapeDtypeStruct((nb, D_MODEL), F32),
            jax.ShapeDtypeStruct((nb, POOL_BUF, W_B), F32),
            jax.ShapeDtypeStruct((nb, W_C, DV_C), F32),
        ],
        scratch_shapes=[
            pltpu.VMEM((nbt, W_B), F32),
            pltpu.VMEM((H_C, nbt, DV_C), F32),
        ],
        name="mix_decode",
        compiler_params=pltpu.CompilerParams(
            dimension_semantics=("arbitrary",), vmem_limit_bytes=VMEM_LIMIT),
    )(x, oa, u, gt, kkt, qht, vc, sg, sp, sh, pw, ps, hg, wo)


def _block_ones(n, blk):
    idx = np.arange(n) // blk
    return jnp.asarray((idx[:, None] == idx[None, :]).astype(np.float32), dtype=BF16)


def _chunk_lower_tri(n, blk):
    idx = np.arange(n)
    same = (idx[:, None] // blk) == (idx[None, :] // blk)
    return jnp.asarray((same & (idx[:, None] >= idx[None, :])).astype(np.float32), dtype=BF16)


def kernel(x_prompt, x_sample, cache_k, cache_v, state_pool, state_hgrn, page_table,
           ln_gain, w_in, qn_gain, kn_gain, lam_q1, lam_k1, lam_q2, lam_k2, subln_gain,
           pool_w, pool_scale, hgrn_lb, hgrn_norm_gain, w_out):
    nb, seq, _ = x_prompt.shape
    t = nb * seq
    db = x_sample.shape[0]
    n_pages = page_table.shape[1]
    past = n_pages * PAGE_SIZE
    n_phys = cache_k.shape[1]

    tm = min(512, t)
    blk = min(512, seq)
    tc = min(256, seq)
    nbt = min(16, db)

    w_in_b = w_in.astype(BF16)
    w_out_b = w_out.astype(BF16)
    sm = jax.nn.softmax(hgrn_lb.astype(F32), axis=0)
    lb_all = jnp.cumsum(sm, axis=0) - sm[0]
    lam_init = [0.8 - 0.6 * math.exp(-0.3 * l) for l in range(DEPTH)]
    lam_all = (jnp.exp(jnp.sum(lam_q1.astype(F32) * lam_k1.astype(F32), axis=-1))
               - jnp.exp(jnp.sum(lam_q2.astype(F32) * lam_k2.astype(F32), axis=-1))
               + jnp.asarray(lam_init, F32))
    slopes_np = np.asarray([2.0 ** (-8.0 * (h + 1) / H_A) for h in range(H_A)], np.float32)
    slopes = jnp.asarray(slopes_np)
    slope_rows = jnp.asarray(slopes_np[np.arange(ATT_ROWS) % H_A].reshape(ATT_ROWS, 1))
    qg = jnp.tile(qn_gain.astype(F32), (1, QK_A // DK_A))
    kg = jnp.tile(kn_gain.astype(F32), (1, QK_A // DK_A))
    hg_tiled = jnp.tile(hgrn_norm_gain.astype(F32), (1, H_C))
    eye_g = jnp.eye(G_B, dtype=F32)
    pw_bd = jnp.einsum('lgcd,gh->lgchd', pool_w.astype(F32), eye_g).reshape(DEPTH, W_B, W_B).astype(BF16)
    bd_qk = _block_ones(QK_A, DK_A)
    hh = _block_ones(W_C, DV_C)
    ltri = _chunk_lower_tri(tc, HGRN_CHUNK)
    ck = cache_k.reshape(DEPTH, n_phys, PAGE_SIZE * H_A, 2 * DK_A)
    cv = cache_v.reshape(DEPTH, n_phys, PAGE_SIZE * H_A, DV_A)
    pt_flat = page_table.reshape(-1).astype(jnp.int32)

    xp = x_prompt.reshape(t, D_MODEL)
    xs = x_sample.reshape(db, D_MODEL)
    ks, vs, pools, states = [], [], [], []
    dks, dvs, dpools, dstates = [], [], [], []
    for l in range(DEPTH):
        lng = ln_gain[l].reshape(1, D_MODEL).astype(F32)
        lb = lb_all[l].reshape(1, W_C)
        lam = lam_all[l].reshape(1)
        sub = subln_gain[l].reshape(1, DV_A).astype(F32)
        ps = pool_scale[l].reshape(1, W_B).astype(F32)
        post = 1.0 - lam_init[l]

        q, k, kb, v, vb, u, g, kk, vc, qh, sg = _inproj_prompt(
            xp, lng, w_in_b[l], qg[l:l + 1], kg[l:l + 1], lb, bd_qk, tm)
        oa = _attn_prompt(lam, slopes, q, kb, vb, sub, nb, seq, blk, post)
        xp, pool_new, st = _mix_prompt(xp, oa, u, g, kk, vc, qh, sg, ltri, hh, pw_bd[l], ps,
                                       hg_tiled[l:l + 1], w_out_b[l], nb, seq, tc)
        ks.append(k)
        vs.append(v)
        pools.append(pool_new)
        states.append(st)

        dq, dk, dv, du, dgt, dkkt, dvc, dqht, dsg = _inproj_decode(
            xs, lng, w_in_b[l], qg[l:l + 1], kg[l:l + 1], lb, bd_qk)
        doa = _attn_paged(pt_flat, lam, slope_rows, dq, dk, dv, sub, ck, cv, l, n_pages, post)
        xs, dpool, dstate = _mix_decode(
            xs, doa.reshape(db, W_A), du, dgt, dkkt, dqht, dvc, dsg, state_pool[l],
            state_hgrn[l].reshape(db, W_C, DV_C), pw_bd[l], ps,
            hgrn_norm_gain[l].reshape(1, DV_C).astype(F32), w_out_b[l], nbt, past)
        dks.append(dk)
        dvs.append(dv)
        dpools.append(dpool)
        dstates.append(dstate)

    y_prompt = xp.reshape(nb, seq, D_MODEL)
    y_sample = xs.reshape(db, 1, D_MODEL)
    k_prompt = jnp.stack(ks).reshape(DEPTH, nb, seq, H_A, 2 * DK_A)
    v_prompt = jnp.stack(vs).reshape(DEPTH, nb, seq, H_A, DV_A)
    k_sample = jnp.stack(dks).reshape(DEPTH, db, 1, H_A, 2 * DK_A)
    v_sample = jnp.stack(dvs).reshape(DEPTH, db, 1, H_A, DV_A)
    pool_prompt = jnp.stack(pools)
    pool_sample = jnp.stack(dpools)
    st_all = jnp.stack(states).reshape(DEPTH, nb, H_C, DV_C, H_C, DK_C)
    hgrn_prompt = jnp.stack([st_all[:, :, h, :, h, :] for h in range(H_C)], axis=2).swapaxes(-1, -2)
    hgrn_sample = jnp.stack(dstates).reshape(DEPTH, db, H_C, DK_C, DV_C)
    return (y_prompt, y_sample, k_prompt, v_prompt, k_sample, v_sample,
            pool_prompt, pool_sample, hgrn_prompt, hgrn_sample)
```

```python
import functools
import math

import numpy as np
import jax
import jax.numpy as jnp
from jax import lax
from jax.experimental import pallas as pl
from jax.experimental.pallas import tpu as pltpu

F32 = jnp.float32
BF16 = jnp.bfloat16

D_MODEL = 1024
DEPTH = 4
PAGE_SIZE = 128
H_A = 4
DK_A = 64
DV_A = 128
W_A = H_A * DV_A
QK_A = H_A * 2 * DK_A
G_B = 4
POOL_WINDOWS = (2, 4, 8, 16)
W_B = 256
C_B = W_B // G_B
POOL_BUF = max(POOL_WINDOWS) - 1
H_C = 4
W_C = 256
DK_C = 64
DV_C = 64
D_IN = 2 * QK_A + W_A + W_B + 3 * W_C + D_MODEL
HGRN_CHUNK = 64
HGRN_SUB = 16
HGRN_SAFE_EXP = 60.0
ATTN_SAFE_SHIFT = 20.0
EPS = 1e-6
NEG = -1e30
assert POOL_WINDOWS == tuple(2 ** (k + 1) for k in range(len(POOL_WINDOWS)))

C_Q, C_K, C_V, C_U = 0, QK_A, 2 * QK_A, 2 * QK_A + W_A
C_F = C_U + W_B
C_I = C_F + W_C
C_QC = C_I + W_C
C_GATE = C_QC + W_C

V7X_VMEM_BYTES = 64 * 1024 * 1024
VMEM_LIMIT = 56 * 1024 * 1024


def _dot(a, b):
    return jnp.dot(a, b, preferred_element_type=F32)


def _dot_nt(a, b):
    return lax.dot_general(a, b, (((1,), (1,)), ((), ())), preferred_element_type=F32)


def _dot_tn(a, b):
    return lax.dot_general(a, b, (((0,), (0,)), ((), ())), preferred_element_type=F32)


def _split2(x):
    hi = x.astype(BF16)
    lo = (x - hi.astype(F32)).astype(BF16)
    return hi, lo


def _split3(x):
    hi = x.astype(BF16)
    r = x - hi.astype(F32)
    mid = r.astype(BF16)
    lo = (r - mid.astype(F32)).astype(BF16)
    return hi, mid, lo


def _group_sum(x, ones_bd):
    hi, lo = _split2(x)
    return _dot(hi, ones_bd) + _dot(lo, ones_bd)


def _sigmoid(x):
    return 1.0 / (1.0 + jnp.exp(-x))


def _inproj_body(x_ref, lng_ref, w_ref, qg_ref, kg_ref, lb_ref, bd_ref):
    x = x_ref[...]
    h = x * lax.rsqrt(jnp.mean(x * x, axis=-1, keepdims=True) + EPS) * lng_ref[...]
    hb = h.astype(BF16)
    bd = bd_ref[...]

    def proj(lo, hi):
        return _dot(hb, w_ref[:, lo:hi])

    def head_norm(z, gain):
        ss = _group_sum(z * z, bd)
        return z * lax.rsqrt(ss * (1.0 / DK_A) + EPS) * gain

    qn = head_norm(proj(C_Q, C_K), qg_ref[...]) * (DK_A ** -0.5)
    kn = head_norm(proj(C_K, C_V), kg_ref[...])
    v = proj(C_V, C_U)
    u = proj(C_U, C_F)
    lb = lb_ref[...]
    sig = _sigmoid(proj(C_F, C_I))
    g = jnp.log(lb + (1.0 - lb) * sig)
    kk = (1.0 - lb) * (1.0 - sig)
    vc = proj(C_I, C_QC)
    qc = proj(C_QC, C_GATE)
    qh = qc * _sigmoid(qc)
    gate = proj(C_GATE, D_IN)
    sg = gate * _sigmoid(gate)
    return qn, kn, v, u, g, kk, vc, qh, sg


def _inproj_prompt_kernel(x_ref, lng_ref, w_ref, qg_ref, kg_ref, lb_ref, bd_ref,
                          q_ref, k_ref, kb_ref, v_ref, vb_ref, u_ref, g_ref, kk_ref, vc_ref,
                          qh_ref, sg_ref):
    qn, kn, v, u, g, kk, vc, qh, sg = _inproj_body(x_ref, lng_ref, w_ref, qg_ref, kg_ref, lb_ref, bd_ref)
    q_ref[...] = qn.astype(BF16)
    k_ref[...] = kn
    kb_ref[...] = kn.astype(BF16)
    v_ref[...] = v
    vb_ref[...] = v.astype(BF16)
    u_ref[...] = u
    g_ref[...] = g
    kk_ref[...] = kk
    vc_ref[...] = vc
    qh_ref[...] = qh
    sg_ref[...] = sg


def _inproj_decode_kernel(x_ref, lng_ref, w_ref, qg_ref, kg_ref, lb_ref, bd_ref,
                          q_ref, k_ref, v_ref, u_ref, gt_ref, kkt_ref, vc_ref, qht_ref, sg_ref):
    qn, kn, v, u, g, kk, vc, qh, sg = _inproj_body(x_ref, lng_ref, w_ref, qg_ref, kg_ref, lb_ref, bd_ref)
    q_ref[...] = qn
    k_ref[...] = kn
    v_ref[...] = v
    u_ref[...] = u
    gt_ref[...] = g.T
    kkt_ref[...] = kk.T
    vc_ref[...] = vc
    qht_ref[...] = qh.T
    sg_ref[...] = sg


def _inproj_in_specs(tm):
    row = lambda i: (i, 0)
    fix = lambda i: (0, 0)
    return [
        pl.BlockSpec((tm, D_MODEL), row),
        pl.BlockSpec((1, D_MODEL), fix),
        pl.BlockSpec((D_MODEL, D_IN), fix),
        pl.BlockSpec((1, QK_A), fix),
        pl.BlockSpec((1, QK_A), fix),
        pl.BlockSpec((1, W_C), fix),
        pl.BlockSpec((QK_A, QK_A), fix),
    ]


def _inproj_prompt(x, lng, w, qg, kg, lb, bd, tm):
    t = x.shape[0]
    row = lambda i: (i, 0)
    widths = [(QK_A, BF16), (QK_A, F32), (QK_A, BF16), (W_A, F32), (W_A, BF16), (W_B, F32),
              (W_C, F32), (W_C, F32), (W_C, F32), (W_C, F32), (D_MODEL, F32)]
    return pl.pallas_call(
        _inproj_prompt_kernel,
        grid=(t // tm,),
        in_specs=_inproj_in_specs(tm),
        out_specs=[pl.BlockSpec((tm, wd), row) for wd, _ in widths],
        out_shape=[jax.ShapeDtypeStruct((t, wd), dt) for wd, dt in widths],
        name="inproj_prompt",
        compiler_params=pltpu.CompilerParams(
            dimension_semantics=("parallel",), vmem_limit_bytes=VMEM_LIMIT),
    )(x, lng, w, qg, kg, lb, bd)


def _inproj_decode(x, lng, w, qg, kg, lb, bd):
    t = x.shape[0]
    row = lambda i: (i, 0)
    shapes = [(t, QK_A), (t, QK_A), (t, W_A), (t, W_B), (W_C, t), (W_C, t), (t, W_C), (W_C, t),
              (t, D_MODEL)]
    return pl.pallas_call(
        _inproj_decode_kernel,
        grid=(1,),
        in_specs=_inproj_in_specs(t),
        out_specs=[pl.BlockSpec(s, row) for s in shapes],
        out_shape=[jax.ShapeDtypeStruct(s, F32) for s in shapes],
        name="inproj_decode",
        compiler_params=pltpu.CompilerParams(
            dimension_semantics=("arbitrary",), vmem_limit_bytes=VMEM_LIMIT),
    )(x, lng, w, qg, kg, lb, bd)


def _attn_kernel(par_ref, slope_ref, q_ref, k_ref, v_ref, kf_ref, sub_ref, o_ref, m_sc, l_sc, acc_sc,
                 *, blk, post_scale):
    del kf_ref
    hd = pl.program_id(1)
    qi = pl.program_id(2)
    slope = slope_ref[hd]
    lam = par_ref[0]
    q = q_ref[...]
    lane = lax.broadcasted_iota(jnp.int32, q.shape, 1)
    zero = jnp.zeros_like(q)
    qm = (jnp.where(lane < DK_A, q, zero), jnp.where(lane >= DK_A, q, zero))
    rel = (lax.broadcasted_iota(jnp.int32, (blk, blk), 0)
           - lax.broadcasted_iota(jnp.int32, (blk, blk), 1))
    srel = slope * rel.astype(F32)
    m_sc[...] = jnp.full(m_sc.shape, NEG, F32)
    l_sc[...] = jnp.zeros(l_sc.shape, F32)
    acc_sc[...] = jnp.zeros(acc_sc.shape, F32)

    def step(kj, masked):
        start = pl.multiple_of(kj * blk, blk)
        kb = k_ref[pl.ds(start, blk), :]
        vb = v_ref[pl.ds(start, blk), :]
        off = slope * ((qi - kj) * blk).astype(F32)
        for c in range(2):
            s = _dot_nt(qm[c], kb) - srel - off
            if masked:
                s = jnp.where(rel >= 0, s, NEG)
            m_old = m_sc[c]
            m_new = jnp.maximum(m_old, jnp.max(s, axis=-1, keepdims=True))
            alpha = jnp.exp(m_old - m_new)
            p = jnp.exp(s - m_new)
            l_sc[c] = alpha * l_sc[c] + jnp.sum(p, axis=-1, keepdims=True)
            acc_sc[c] = alpha * acc_sc[c] + _dot(p.astype(BF16), vb)
            m_sc[c] = m_new

    def body(kj, carry):
        step(kj, False)
        return carry

    lax.fori_loop(0, qi, body, 0)
    step(qi, True)
    o = acc_sc[0] / l_sc[0] - lam * (acc_sc[1] / l_sc[1])
    on = o * lax.rsqrt(jnp.mean(o * o, axis=-1, keepdims=True) + EPS) * sub_ref[...]
    o_ref[...] = on * post_scale


N_SHIFT_PARTS = 3
FEAT_SUB = 64


def _attn_k_features(blk):
    j = np.arange(blk)
    f = np.zeros((blk, DV_A), np.float32)
    f[:, 0] = 1.0
    f[:, 1] = 1.0
    f[:, 2] = j // FEAT_SUB
    f[:, 3] = j % FEAT_SUB
    f[:, 4:4 + N_SHIFT_PARTS] = 1.0
    return jnp.asarray(f, dtype=BF16)


def _attn_shift_kernel(par_ref, slope_ref, q_ref, k_ref, v_ref, kf_ref, sub_ref, o_ref, acc_sc,
                       *, blk, post_scale):
    hd = pl.program_id(1)
    qi = pl.program_id(2)
    slope = slope_ref[hd]
    lam = par_ref[0]
    q = q_ref[...]
    lane = lax.broadcasted_iota(jnp.int32, q.shape, 1)
    row = lax.broadcasted_iota(jnp.int32, q.shape, 0)
    ih = (row // FEAT_SUB).astype(F32)
    il = (row % FEAT_SUB).astype(F32)
    qf = jnp.where(lane == 0, -(slope * FEAT_SUB) * ih, 0.0)
    qf = jnp.where(lane == 1, -slope * il, qf)
    qf = jnp.where(lane == 2, slope * FEAT_SUB, qf)
    qf = jnp.where(lane == 3, slope, qf)
    for part in range(N_SHIFT_PARTS):
        qf = jnp.where(lane == 4 + part, -par_ref[1 + part], qf)
    qfb = qf.astype(BF16)
    zero = jnp.zeros_like(q)
    qe = (jnp.concatenate([jnp.where(lane < DK_A, q, zero), qfb], axis=1),
          jnp.concatenate([jnp.where(lane >= DK_A, q, zero), qfb], axis=1))
    kf = kf_ref[...]
    ones = jnp.ones((blk, DV_A), BF16)
    acc_sc[...] = jnp.zeros(acc_sc.shape, F32)

    def step(kj, masked):
        start = pl.multiple_of(kj * blk, blk)
        kb = jnp.concatenate([k_ref[pl.ds(start, blk), :], kf], axis=1)
        vb = jnp.concatenate([v_ref[pl.ds(start, blk), :], ones], axis=1)
        cstep = slope * ((qi - kj) * blk).astype(F32)
        for c in range(2):
            p = jnp.exp(_dot_nt(qe[c], kb) - cstep)
            if masked:
                rel = (lax.broadcasted_iota(jnp.int32, (blk, blk), 0)
                       - lax.broadcasted_iota(jnp.int32, (blk, blk), 1))
                p = jnp.where(rel >= 0, p, 0.0)
            acc_sc[c] += _dot(p.astype(BF16), vb)

    def pair(i, carry):
        step(2 * i, False)
        step(2 * i + 1, False)
        return carry

    lax.fori_loop(0, qi // 2, pair, 0)

    @pl.when(qi % 2 == 1)
    def _():
        step(qi - 1, False)
        step(qi, True)

    @pl.when(qi % 2 == 0)
    def _():
        step(qi, True)

    a0 = acc_sc[0]
    a1 = acc_sc[1]
    o = a0[:, 0:DV_A] / a0[:, DV_A:] - lam * (a1[:, 0:DV_A] / a1[:, DV_A:])
    on = o * lax.rsqrt(jnp.mean(o * o, axis=-1, keepdims=True) + EPS) * sub_ref[...]
    o_ref[...] = on * post_scale


def _attn_prompt(par, slopes, q, k, v, kf, sub, nb, seq, blk, post_scale, shifted):
    t = nb * seq
    nq = seq // blk
    smem = pl.BlockSpec(memory_space=pltpu.SMEM)
    if shifted:
        body = functools.partial(_attn_shift_kernel, blk=blk, post_scale=post_scale)
        scratch = [pltpu.VMEM((2, blk, 2 * DV_A), F32)]
        name = "attn_prompt_shift"
    else:
        body = functools.partial(_attn_kernel, blk=blk, post_scale=post_scale)
        scratch = [pltpu.VMEM((2, blk, 1), F32), pltpu.VMEM((2, blk, 1), F32),
                   pltpu.VMEM((2, blk, DV_A), F32)]
        name = "attn_prompt_online"
    return pl.pallas_call(
        body,
        grid=(nb, H_A, nq),
        in_specs=[
            smem, smem,
            pl.BlockSpec((blk, DV_A), lambda b, h, i: (b * nq + i, h)),
            pl.BlockSpec((seq, DV_A), lambda b, h, i: (b, h)),
            pl.BlockSpec((seq, DV_A), lambda b, h, i: (b, h)),
            pl.BlockSpec((blk, DV_A), lambda b, h, i: (0, 0)),
            pl.BlockSpec((1, DV_A), lambda b, h, i: (0, 0)),
        ],
        out_specs=pl.BlockSpec((blk, DV_A), lambda b, h, i: (b * nq + i, h)),
        out_shape=jax.ShapeDtypeStruct((t, W_A), F32),
        scratch_shapes=scratch,
        name=name,
        compiler_params=pltpu.CompilerParams(
            dimension_semantics=("parallel", "parallel", "arbitrary"),
            vmem_limit_bytes=VMEM_LIMIT),
    )(par, slopes, q, k, v, kf, sub)


def _block_diag_rows(m):
    head = lax.broadcasted_iota(jnp.int32, m.shape, 1) // DK_C
    parts = [jnp.where(head == hp, m, 0.0).astype(BF16) for hp in range(H_C)]
    return jnp.concatenate(parts, axis=0)


def _mix_kernel(x_ref, oa_ref, u_ref, g_ref, kk_ref, vc_ref, qh_ref, sg_ref,
                ltri_ref, hh_ref, pw_ref, ps_ref, hg_ref, wo_ref,
                xo_ref, pool_ref, hst_ref,
                ubuf, kkbuf, gbuf, vcbuf, st_sc, oc_sc, *, tc):
    si = pl.program_id(1)
    ns = pl.num_programs(1)
    pad = HGRN_SUB
    nchunk = tc // HGRN_CHUNK

    @pl.when(si == 0)
    def _():
        st_sc[...] = jnp.zeros(st_sc.shape, F32)
        z = jnp.zeros((pad, W_B), F32)
        ubuf[0:pad, :] = z
        kkbuf[0:pad, :] = z
        gbuf[0:pad, :] = z
        vcbuf[0:pad, :] = z

    u = u_ref[...]
    ubuf[pad:pad + tc, :] = u
    acc = ubuf[...]
    wsum = {}
    for k, win in enumerate(POOL_WINDOWS):
        acc = acc + pltpu.roll(acc, 2 ** k, 0)
        wsum[win] = acc[pad:, :]
    pos = (si * tc + lax.broadcasted_iota(jnp.int32, (tc, 1), 0)).astype(F32)
    lane_b = lax.broadcasted_iota(jnp.int32, (tc, W_B), 1)
    pooled = None
    for gi, win in reversed(list(enumerate(POOL_WINDOWS))):
        term = wsum[win] * (1.0 / jnp.minimum(pos + 1.0, float(win)))
        pooled = term if pooled is None else jnp.where(lane_b < (gi + 1) * C_B, term, pooled)
    pooled = pooled - u
    ob = _dot(pooled.astype(BF16), pw_ref[...]) * ps_ref[...]
    ubuf[0:pad, :] = ubuf[tc:tc + pad, :]

    g = g_ref[...]
    kk = kk_ref[...]
    vc = vc_ref[...]
    qh = qh_ref[...]
    ltri = ltri_ref[...]
    ghi, gmid, glo = _split3(g)
    gc = _dot(ltri, ghi) + _dot(ltri, gmid) + _dot(ltri, glo)
    gc3 = gc.reshape(nchunk, HGRN_CHUNK, W_C)

    def chunk_row(idx):
        r = jnp.broadcast_to(gc3[:, idx:idx + 1, :], (nchunk, HGRN_CHUNK, W_C))
        return r.reshape(tc, W_C)

    rc = lax.broadcasted_iota(jnp.int32, (tc, 1), 0) % HGRN_CHUNK
    eg = jnp.exp(gc)
    qe = qh * eg
    kdec = kk * jnp.exp(chunk_row(HGRN_CHUNK - 1) - gc)
    hh = hh_ref[...]
    t_idx = lax.broadcasted_iota(jnp.int32, (HGRN_CHUNK, H_C * HGRN_CHUNK), 0)
    s_idx = lax.broadcasted_iota(jnp.int32, (HGRN_CHUNK, H_C * HGRN_CHUNK), 1) % HGRN_CHUNK
    chunks = [slice(c * HGRN_CHUNK, (c + 1) * HGRN_CHUNK) for c in range(nchunk)]

    rmid = chunk_row(HGRN_CHUNK // 2 - 1)
    safe = jnp.max(jnp.abs(gc - rmid)) <= HGRN_SAFE_EXP

    @pl.when(safe)
    def _():
        qt = qh * jnp.exp(gc - rmid)
        kt = kk * jnp.exp(rmid - gc)
        for sl in chunks:
            a = _dot_nt(qt[sl].astype(BF16), _block_diag_rows(kt[sl]))
            a = jnp.where(t_idx >= s_idx, a, 0.0)
            oc_sc[sl, :] = _dot(a.astype(BF16), _block_diag_rows(vc[sl]))

    @pl.when(jnp.logical_not(safe))
    def _():
        upper = rc >= 32
        r31 = chunk_row(31)
        qt1 = jnp.where(upper, qh * jnp.exp(jnp.minimum(gc - r31, 0.0)), 0.0)
        kt1 = jnp.where(upper, 0.0, kk * jnp.exp(jnp.minimum(r31 - gc, 0.0)))
        ref2 = jnp.where(upper, chunk_row(47), chunk_row(15))
        odd = ((rc // HGRN_SUB) % 2) == 1
        qt2 = jnp.where(odd, qh * jnp.exp(jnp.minimum(gc - ref2, 0.0)), 0.0)
        kt2 = jnp.where(odd, 0.0, kk * jnp.exp(jnp.minimum(ref2 - gc, 0.0)))
        tb = t_idx // HGRN_SUB
        mask2 = ((tb % 2) == 1) & (s_idx // HGRN_SUB == tb - 1)

        kkbuf[pad:pad + tc, :] = kk
        gbuf[pad:pad + tc, :] = gc
        vcbuf[pad:pad + tc, :] = vc
        r16 = rc % HGRN_SUB
        od = jnp.zeros((tc, W_C), F32)
        for d in range(HGRN_SUB):
            kks = kkbuf[pad - d:pad - d + tc, :]
            gs = gbuf[pad - d:pad - d + tc, :]
            vcs = vcbuf[pad - d:pad - d + tc, :]
            xd = jnp.where(r16 >= d, qh * kks * jnp.exp(jnp.minimum(gc - gs, 0.0)), 0.0)
            od = od + _dot(xd.astype(BF16), hh) * vcs
        for sl in chunks:
            a1 = _dot_nt(qt1[sl].astype(BF16), _block_diag_rows(kt1[sl]))
            a2 = _dot_nt(qt2[sl].astype(BF16), _block_diag_rows(kt2[sl]))
            a_off = a1 + jnp.where(mask2, a2, 0.0)
            oc_sc[sl, :] = _dot(a_off.astype(BF16), _block_diag_rows(vc[sl])) + od[sl]

    bdmask = (lax.broadcasted_iota(jnp.int32, (W_C, W_C), 0) // DK_C
              == lax.broadcasted_iota(jnp.int32, (W_C, W_C), 1) // DK_C)
    for c, sl in enumerate(chunks):
        st = st_sc[...]
        oc_sc[sl, :] += _dot_nt(qe[sl].astype(BF16), st.astype(BF16))
        last = c * HGRN_CHUNK + HGRN_CHUNK - 1
        upd = _dot_tn(vc[sl].astype(BF16), kdec[sl].astype(BF16))
        st_sc[...] = st * eg[last:last + 1, :] + jnp.where(bdmask, upd, 0.0)

    o = oc_sc[...]
    ss = _group_sum(o * o, hh)
    ocn = o * lax.rsqrt(ss * (1.0 / DV_C) + EPS) * hg_ref[...]

    sg = sg_ref[...]
    y = (_dot((oa_ref[...] * sg[:, 0:W_A]).astype(BF16), wo_ref[0:W_A, :])
         + _dot((ob * sg[:, W_A:W_A + W_B]).astype(BF16), wo_ref[W_A:W_A + W_B, :])
         + _dot((ocn * sg[:, W_A + W_B:]).astype(BF16), wo_ref[W_A + W_B:, :]))
    xo_ref[...] = x_ref[...] + y

    @pl.when(si == ns - 1)
    def _():
        pool_ref[0] = u[tc - POOL_BUF:, :]
        hst_ref[0] = st_sc[...]


def _mix_prompt(x, oa, u, g, kk, vc, qh, sg, ltri, hh, pw, ps, hg, wo, nb, seq, tc):
    t = nb * seq
    ns = seq // tc
    row = lambda b, s: (b * ns + s, 0)
    fix = lambda b, s: (0, 0)
    in_specs = [pl.BlockSpec((tc, D_MODEL), row), pl.BlockSpec((tc, W_A), row)]
    in_specs += [pl.BlockSpec((tc, W_C), row) for _ in range(5)]
    in_specs += [
        pl.BlockSpec((tc, D_MODEL), row),
        pl.BlockSpec((tc, tc), fix),
        pl.BlockSpec((W_C, W_C), fix),
        pl.BlockSpec((W_B, W_B), fix),
        pl.BlockSpec((1, W_B), fix),
        pl.BlockSpec((1, W_C), fix),
        pl.BlockSpec((D_MODEL, D_MODEL), fix),
    ]
    return pl.pallas_call(
        functools.partial(_mix_kernel, tc=tc),
        grid=(nb, ns),
        in_specs=in_specs,
        out_specs=[
            pl.BlockSpec((tc, D_MODEL), row),
            pl.BlockSpec((1, POOL_BUF, W_B), lambda b, s: (b, 0, 0)),
            pl.BlockSpec((1, W_C, W_C), lambda b, s: (b, 0, 0)),
        ],
        out_shape=[
            jax.ShapeDtypeStruct((t, D_MODEL), F32),
            jax.ShapeDtypeStruct((nb, POOL_BUF, W_B), F32),
            jax.ShapeDtypeStruct((nb, W_C, W_C), F32),
        ],
        scratch_shapes=[
            pltpu.VMEM((tc + HGRN_SUB, W_B), F32),
            pltpu.VMEM((tc + HGRN_SUB, W_C), F32),
            pltpu.VMEM((tc + HGRN_SUB, W_C), F32),
            pltpu.VMEM((tc + HGRN_SUB, W_C), F32),
            pltpu.VMEM((W_C, W_C), F32),
            pltpu.VMEM((tc, W_C), F32),
        ],
        name="mix_prompt",
        compiler_params=pltpu.CompilerParams(
            dimension_semantics=("parallel", "arbitrary"), vmem_limit_bytes=VMEM_LIMIT),
    )(x, oa, u, g, kk, vc, qh, sg, ltri, hh, pw, ps, hg, wo)


ATT_ROWS = 16


def _paged_attn_kernel(pt_ref, lam_ref, slope_ref, q_ref, kn_ref, vn_ref, sub_ref, *rest,
                       n_pages, post_scale):
    del pt_ref
    k_refs = rest[:n_pages]
    v_refs = rest[n_pages:2 * n_pages]
    o_ref = rest[2 * n_pages]
    s_sc = rest[2 * n_pages + 1]
    rows = PAGE_SIZE * H_A
    past = n_pages * PAGE_SIZE
    lam = lam_ref[0]

    r = lax.broadcasted_iota(jnp.int32, (ATT_ROWS, DV_A), 0)
    lane = lax.broadcasted_iota(jnp.int32, (ATT_ROWS, DV_A), 1)

    def head_rows(row):
        out = jnp.zeros((ATT_ROWS, DV_A), F32)
        for h in range(H_A):
            piece = jnp.broadcast_to(row[:, h * DV_A:(h + 1) * DV_A], (ATT_ROWS, DV_A))
            out = jnp.where(r % H_A == h, piece, out)
        return out

    q = jnp.where(lane // DK_A == r // (ATT_ROWS // 2), head_rows(q_ref[0]), 0.0)
    kn = head_rows(kn_ref[0])
    vn = head_rows(vn_ref[0])
    qb = q.astype(BF16)
    slope = slope_ref[...]

    col = lax.broadcasted_iota(jnp.int32, (ATT_ROWS, rows), 1)
    rr = lax.broadcasted_iota(jnp.int32, (ATT_ROWS, rows), 0)
    valid = (col % H_A) == (rr % H_A)
    tok = col // H_A
    for j in range(n_pages):
        s = _dot_nt(qb, k_refs[j][...].astype(BF16))
        dist = (past - j * PAGE_SIZE - tok).astype(F32)
        s_sc[:, j * rows:(j + 1) * rows] = jnp.where(valid, s - slope * dist, NEG)
    s_self = jnp.sum(q * kn, axis=-1, keepdims=True)
    s_all = s_sc[...]
    m = jnp.maximum(jnp.max(s_all, axis=-1, keepdims=True), s_self)
    p_self = jnp.exp(s_self - m)
    l = p_self
    acc = p_self * vn
    for j in range(n_pages):
        p = jnp.exp(s_all[:, j * rows:(j + 1) * rows] - m)
        l = l + jnp.sum(p, axis=-1, keepdims=True)
        acc = acc + _dot(p.astype(BF16), v_refs[j][...].astype(BF16))
    o16 = acc / l
    half = ATT_ROWS // 2
    o = o16[0:half] - lam * o16[half:ATT_ROWS]
    on = o * lax.rsqrt(jnp.mean(o * o, axis=-1, keepdims=True) + EPS) * sub_ref[...]
    o_ref[0] = (on * post_scale)[0:H_A]


def _attn_paged(pt_flat, lam, slope_rows, q, kn, vn, sub, ck, cv, layer, n_pages, post_scale):
    nb = q.shape[0]
    rows = PAGE_SIZE * H_A
    smem = pl.BlockSpec(memory_space=pltpu.SMEM)
    row3 = lambda b, pt: (b, 0, 0)

    def page_spec(j):
        return pl.BlockSpec((None, None, rows, DV_A),
                            lambda b, pt, j=j: (layer, pt[b * n_pages + j], 0, 0))

    in_specs = [smem,
                pl.BlockSpec((ATT_ROWS, 1), lambda b, pt: (0, 0)),
                pl.BlockSpec((1, 1, QK_A), row3),
                pl.BlockSpec((1, 1, QK_A), row3),
                pl.BlockSpec((1, 1, W_A), row3),
                pl.BlockSpec((1, DV_A), lambda b, pt: (0, 0))]
    in_specs += [page_spec(j) for j in range(n_pages)] * 2
    grid_spec = pltpu.PrefetchScalarGridSpec(
        num_scalar_prefetch=1,
        grid=(nb,),
        in_specs=in_specs,
        out_specs=pl.BlockSpec((1, H_A, DV_A), row3),
        scratch_shapes=[pltpu.VMEM((ATT_ROWS, n_pages * rows), F32)],
    )
    return pl.pallas_call(
        functools.partial(_paged_attn_kernel, n_pages=n_pages, post_scale=post_scale),
        grid_spec=grid_spec,
        out_shape=jax.ShapeDtypeStruct((nb, H_A, DV_A), F32),
        name="attn_paged",
        compiler_params=pltpu.CompilerParams(
            dimension_semantics=("arbitrary",), vmem_limit_bytes=VMEM_LIMIT),
    )(pt_flat, lam, slope_rows, q.reshape(nb, 1, QK_A), kn.reshape(nb, 1, QK_A),
      vn.reshape(nb, 1, W_A), sub, *([ck] * n_pages), *([cv] * n_pages))


def _dec_mix_kernel(x_ref, oa_ref, u_ref, gt_ref, kkt_ref, qht_ref, vc_ref, sg_ref, sp_ref, sh_ref,
                    pw_ref, ps_ref, hg_ref, wo_ref,
                    xo_ref, po_ref, ho_ref, pooled_sc, o_sc, *, nbt, past):
    i = pl.program_id(0)
    ridx = lax.broadcasted_iota(jnp.int32, (POOL_BUF, W_B), 0)
    lane_b = lax.broadcasted_iota(jnp.int32, (1, W_B), 1)
    lane_t = lax.broadcasted_iota(jnp.int32, gt_ref.shape, 1)

    def per_batch(b, carry):
        prev = sp_ref[b]
        urow = u_ref[pl.ds(b, 1), :]
        pooled = None
        for gi, win in reversed(list(enumerate(POOL_WINDOWS))):
            ws = urow + jnp.sum(jnp.where(ridx >= POOL_BUF + 1 - win, prev, 0.0), axis=0, keepdims=True)
            term = ws * (1.0 / min(past + 1, win))
            pooled = term if pooled is None else jnp.where(lane_b < (gi + 1) * C_B, term, pooled)
        pooled_sc[pl.ds(b, 1), :] = pooled - urow
        po_ref[b, 0:POOL_BUF - 1, :] = prev[1:POOL_BUF, :]
        po_ref[b, POOL_BUF - 1:POOL_BUF, :] = urow

        bl = i * nbt + b

        def col(ref):
            return jnp.sum(jnp.where(lane_t == bl, ref[...], 0.0), axis=1, keepdims=True)

        fcol = jnp.exp(col(gt_ref))
        kcol = col(kkt_ref)
        qcol = col(qht_ref)
        vrow = vc_ref[pl.ds(b, 1), :]
        vrows = jnp.concatenate(
            [jnp.broadcast_to(vrow[:, h * DV_C:(h + 1) * DV_C], (DK_C, DV_C)) for h in range(H_C)],
            axis=0)
        s_new = fcol * sh_ref[b] + kcol * vrows
        ho_ref[b] = s_new
        o4 = jnp.sum((s_new * qcol).reshape(H_C, DK_C, DV_C), axis=1)
        for h in range(H_C):
            o_sc[h, pl.ds(b, 1), :] = o4[h:h + 1, :]
        return carry

    lax.fori_loop(0, nbt, per_batch, 0)

    ob = _dot(pooled_sc[...].astype(BF16), pw_ref[...]) * ps_ref[...]
    sg = sg_ref[...]
    y = (_dot((oa_ref[...] * sg[:, 0:W_A]).astype(BF16), wo_ref[0:W_A, :])
         + _dot((ob * sg[:, W_A:W_A + W_B]).astype(BF16), wo_ref[W_A:W_A + W_B, :]))
    for h in range(H_C):
        oh = o_sc[h]
        ohn = oh * lax.rsqrt(jnp.mean(oh * oh, axis=-1, keepdims=True) + EPS) * hg_ref[...]
        lo = W_A + W_B + h * DV_C
        y = y + _dot((ohn * sg[:, lo:lo + DV_C]).astype(BF16), wo_ref[lo:lo + DV_C, :])
    xo_ref[...] = x_ref[...] + y


def _mix_decode(x, oa, u, gt, kkt, qht, vc, sg, sp, sh, pw, ps, hg, wo, nbt, past):
    nb = x.shape[0]
    row = lambda i: (i, 0)
    fix = lambda i: (0, 0)
    row3 = lambda i: (i, 0, 0)
    in_specs = [
        pl.BlockSpec((nbt, D_MODEL), row),
        pl.BlockSpec((nbt, W_A), row),
        pl.BlockSpec((nbt, W_B), row),
        pl.BlockSpec((W_C, nb), fix),
        pl.BlockSpec((W_C, nb), fix),
        pl.BlockSpec((W_C, nb), fix),
        pl.BlockSpec((nbt, W_C), row),
        pl.BlockSpec((nbt, D_MODEL), row),
        pl.BlockSpec((nbt, POOL_BUF, W_B), row3),
        pl.BlockSpec((nbt, W_C, DV_C), row3),
        pl.BlockSpec((W_B, W_B), fix),
        pl.BlockSpec((1, W_B), fix),
        pl.BlockSpec((1, DV_C), fix),
        pl.BlockSpec((D_MODEL, D_MODEL), fix),
    ]
    return pl.pallas_call(
        functools.partial(_dec_mix_kernel, nbt=nbt, past=past),
        grid=(nb // nbt,),
        in_specs=in_specs,
        out_specs=[
            pl.BlockSpec((nbt, D_MODEL), row),
            pl.BlockSpec((nbt, POOL_BUF, W_B), row3),
            pl.BlockSpec((nbt, W_C, DV_C), row3),
        ],
        out_shape=[
            jax.ShapeDtypeStruct((nb, D_MODEL), F32),
            jax.ShapeDtypeStruct((nb, POOL_BUF, W_B), F32),
            jax.ShapeDtypeStruct((nb, W_C, DV_C), F32),
        ],
        scratch_shapes=[
            pltpu.VMEM((nbt, W_B), F32),
            pltpu.VMEM((H_C, nbt, DV_C), F32),
        ],
        name="mix_decode",
        compiler_params=pltpu.CompilerParams(
            dimension_semantics=("arbitrary",), vmem_limit_bytes=VMEM_LIMIT),
    )(x, oa, u, gt, kkt, qht, vc, sg, sp, sh, pw, ps, hg, wo)


def _block_ones(n, blk):
    idx = np.arange(n) // blk
    return jnp.asarray((idx[:, None] == idx[None, :]).astype(np.float32), dtype=BF16)


def _chunk_lower_tri(n, blk):
    idx = np.arange(n)
    same = (idx[:, None] // blk) == (idx[None, :] // blk)
    return jnp.asarray((same & (idx[:, None] >= idx[None, :])).astype(np.float32), dtype=BF16)


def kernel(x_prompt, x_sample, cache_k, cache_v, state_pool, state_hgrn, page_table,
           ln_gain, w_in, qn_gain, kn_gain, lam_q1, lam_k1, lam_q2, lam_k2, subln_gain,
           pool_w, pool_scale, hgrn_lb, hgrn_norm_gain, w_out):
    nb, seq, _ = x_prompt.shape
    t = nb * seq
    db = x_sample.shape[0]
    n_pages = page_table.shape[1]
    past = n_pages * PAGE_SIZE
    n_phys = cache_k.shape[1]

    tm = min(512, t)
    blk = min(512, seq)
    tc = min(256, seq)
    nbt = min(16, db)

    w_in_b = w_in.astype(BF16)
    w_out_b = w_out.astype(BF16)
    sm = jax.nn.softmax(hgrn_lb.astype(F32), axis=0)
    lb_all = jnp.cumsum(sm, axis=0) - sm[0]
    lam_init = [0.8 - 0.6 * math.exp(-0.3 * l) for l in range(DEPTH)]
    lam_all = (jnp.exp(jnp.sum(lam_q1.astype(F32) * lam_k1.astype(F32), axis=-1))
               - jnp.exp(jnp.sum(lam_q2.astype(F32) * lam_k2.astype(F32), axis=-1))
               + jnp.asarray(lam_init, F32))
    slopes_np = np.asarray([2.0 ** (-8.0 * (h + 1) / H_A) for h in range(H_A)], np.float32)
    slopes = jnp.asarray(slopes_np)
    slope_rows = jnp.asarray(slopes_np[np.arange(ATT_ROWS) % H_A].reshape(ATT_ROWS, 1))
    score_bound = (1.0125 * DK_A ** 0.5) * jnp.max(
        jnp.abs(qn_gain.astype(F32) * kn_gain.astype(F32)), axis=-1)
    sb_hi = score_bound.astype(BF16).astype(F32)
    sb_mid = (score_bound - sb_hi).astype(BF16).astype(F32)
    sb_lo = (score_bound - sb_hi - sb_mid).astype(BF16).astype(F32)
    attn_par = jnp.stack([lam_all, sb_hi, sb_mid, sb_lo], axis=1)
    kfeat = _attn_k_features(blk)
    qg = jnp.tile(qn_gain.astype(F32), (1, QK_A // DK_A))
    kg = jnp.tile(kn_gain.astype(F32), (1, QK_A // DK_A))
    hg_tiled = jnp.tile(hgrn_norm_gain.astype(F32), (1, H_C))
    eye_g = jnp.eye(G_B, dtype=F32)
    pw_bd = jnp.einsum('lgcd,gh->lgchd', pool_w.astype(F32), eye_g).reshape(DEPTH, W_B, W_B).astype(BF16)
    bd_qk = _block_ones(QK_A, DK_A)
    hh = _block_ones(W_C, DV_C)
    ltri = _chunk_lower_tri(tc, HGRN_CHUNK)
    ck = cache_k.reshape(DEPTH, n_phys, PAGE_SIZE * H_A, 2 * DK_A)
    cv = cache_v.reshape(DEPTH, n_phys, PAGE_SIZE * H_A, DV_A)
    pt_flat = page_table.reshape(-1).astype(jnp.int32)

    xp = x_prompt.reshape(t, D_MODEL)
    xs = x_sample.reshape(db, D_MODEL)
    ks, vs, pools, states = [], [], [], []
    dks, dvs, dpools, dstates = [], [], [], []
    for l in range(DEPTH):
        lng = ln_gain[l].reshape(1, D_MODEL).astype(F32)
        lb = lb_all[l].reshape(1, W_C)
        lam = lam_all[l].reshape(1)
        sub = subln_gain[l].reshape(1, DV_A).astype(F32)
        ps = pool_scale[l].reshape(1, W_B).astype(F32)
        post = 1.0 - lam_init[l]

        q, k, kb, v, vb, u, g, kk, vc, qh, sg = _inproj_prompt(
            xp, lng, w_in_b[l], qg[l:l + 1], kg[l:l + 1], lb, bd_qk, tm)
        oa = lax.cond(
            score_bound[l] <= ATTN_SAFE_SHIFT,
            functools.partial(_attn_prompt, nb=nb, seq=seq, blk=blk, post_scale=post, shifted=True),
            functools.partial(_attn_prompt, nb=nb, seq=seq, blk=blk, post_scale=post, shifted=False),
            attn_par[l], slopes, q, kb, vb, kfeat, sub)
        xp, pool_new, st = _mix_prompt(xp, oa, u, g, kk, vc, qh, sg, ltri, hh, pw_bd[l], ps,
                                       hg_tiled[l:l + 1], w_out_b[l], nb, seq, tc)
        ks.append(k)
        vs.append(v)
        pools.append(pool_new)
        states.append(st)

        dq, dk, dv, du, dgt, dkkt, dvc, dqht, dsg = _inproj_decode(
            xs, lng, w_in_b[l], qg[l:l + 1], kg[l:l + 1], lb, bd_qk)
        doa = _attn_paged(pt_flat, lam, slope_rows, dq, dk, dv, sub, ck, cv, l, n_pages, post)
        xs, dpool, dstate = _mix_decode(
            xs, doa.reshape(db, W_A), du, dgt, dkkt, dqht, dvc, dsg, state_pool[l],
            state_hgrn[l].reshape(db, W_C, DV_C), pw_bd[l], ps,
            hgrn_norm_gain[l].reshape(1, DV_C).astype(F32), w_out_b[l], nbt, past)
        dks.append(dk)
        dvs.append(dv)
        dpools.append(dpool)
        dstates.append(dstate)

    y_prompt = xp.reshape(nb, seq, D_MODEL)
    y_sample = xs.reshape(db, 1, D_MODEL)
    k_prompt = jnp.stack(ks).reshape(DEPTH, nb, seq, H_A, 2 * DK_A)
    v_prompt = jnp.stack(vs).reshape(DEPTH, nb, seq, H_A, DV_A)
    k_sample = jnp.stack(dks).reshape(DEPTH, db, 1, H_A, 2 * DK_A)
    v_sample = jnp.stack(dvs).reshape(DEPTH, db, 1, H_A, DV_A)
    pool_prompt = jnp.stack(pools)
    pool_sample = jnp.stack(dpools)
    st_all = jnp.stack(states).reshape(DEPTH, nb, H_C, DV_C, H_C, DK_C)
    hgrn_prompt = jnp.stack([st_all[:, :, h, :, h, :] for h in range(H_C)], axis=2).swapaxes(-1, -2)
    hgrn_sample = jnp.stack(dstates).reshape(DEPTH, db, H_C, DK_C, DV_C)
    return (y_prompt, y_sample, k_prompt, v_prompt, k_sample, v_sample,
            pool_prompt, pool_sample, hgrn_prompt, hgrn_sample)
```

```python
import functools
import math

import numpy as np
import jax
import jax.numpy as jnp
from jax import lax
from jax.experimental import pallas as pl
from jax.experimental.pallas import tpu as pltpu

F32 = jnp.float32
BF16 = jnp.bfloat16

D_MODEL = 1024
DEPTH = 4
PAGE_SIZE = 128
H_A = 4
DK_A = 64
DV_A = 128
W_A = H_A * DV_A
QK_A = H_A * 2 * DK_A
G_B = 4
POOL_WINDOWS = (2, 4, 8, 16)
W_B = 256
C_B = W_B // G_B
POOL_BUF = max(POOL_WINDOWS) - 1
H_C = 4
W_C = 256
DK_C = 64
DV_C = 64
D_IN = 2 * QK_A + W_A + W_B + 3 * W_C + D_MODEL
HGRN_CHUNK = 64
HGRN_SUB = 16
HGRN_SAFE_EXP = 60.0
ATTN_SAFE_SHIFT = 20.0
EPS = 1e-6
NEG = -1e30
assert POOL_WINDOWS == tuple(2 ** (k + 1) for k in range(len(POOL_WINDOWS)))

C_Q, C_K, C_V, C_U = 0, QK_A, 2 * QK_A, 2 * QK_A + W_A
C_F = C_U + W_B
C_I = C_F + W_C
C_QC = C_I + W_C
C_GATE = C_QC + W_C

V7X_VMEM_BYTES = 64 * 1024 * 1024
VMEM_LIMIT = 56 * 1024 * 1024


def _dot(a, b):
    return jnp.dot(a, b, preferred_element_type=F32)


def _dot_nt(a, b):
    return lax.dot_general(a, b, (((1,), (1,)), ((), ())), preferred_element_type=F32)


def _dot_tn(a, b):
    return lax.dot_general(a, b, (((0,), (0,)), ((), ())), preferred_element_type=F32)


def _split3(x):
    hi = x.astype(BF16)
    r = x - hi.astype(F32)
    mid = r.astype(BF16)
    lo = (r - mid.astype(F32)).astype(BF16)
    return hi, mid, lo


def _sigmoid(x):
    return 1.0 / (1.0 + jnp.exp(-x))


def _inproj_body(x_ref, lng_ref, w_ref, qg_ref, kg_ref, lb_ref, bd_ref):
    x = x_ref[...]
    h = x * lax.rsqrt(jnp.mean(x * x, axis=-1, keepdims=True) + EPS) * lng_ref[...]
    hb = h.astype(BF16)
    bd = bd_ref[...]

    def proj(lo, hi):
        return _dot(hb, w_ref[:, lo:hi])

    def head_norm(z, gain):
        ss = _dot((z * z).astype(BF16), bd)
        return z * lax.rsqrt(ss * (1.0 / DK_A) + EPS) * gain

    qn = head_norm(proj(C_Q, C_K), qg_ref[...]) * (DK_A ** -0.5)
    kn = head_norm(proj(C_K, C_V), kg_ref[...])
    v = proj(C_V, C_U)
    u = proj(C_U, C_F)
    lb = lb_ref[...]
    sig = _sigmoid(proj(C_F, C_I))
    g = jnp.log(lb + (1.0 - lb) * sig)
    kk = (1.0 - lb) * (1.0 - sig)
    vc = proj(C_I, C_QC)
    qc = proj(C_QC, C_GATE)
    qh = qc * _sigmoid(qc)
    gate = proj(C_GATE, D_IN)
    sg = gate * _sigmoid(gate)
    return qn, kn, v, u, g, kk, vc, qh, sg


def _inproj_prompt_kernel(x_ref, lng_ref, w_ref, qg_ref, kg_ref, lb_ref, bd_ref, kall_ref, vall_ref,
                          q_ref, k_ref, kb_ref, v_ref, vb_ref, u_ref, g_ref, kk_ref, vc_ref,
                          qh_ref, sg_ref):
    del kall_ref, vall_ref
    qn, kn, v, u, g, kk, vc, qh, sg = _inproj_body(x_ref, lng_ref, w_ref, qg_ref, kg_ref, lb_ref, bd_ref)
    tm = x_ref.shape[0]
    q_ref[...] = qn.astype(BF16)
    for h in range(H_A):
        k_ref[pl.ds(h, tm, stride=H_A), :] = kn[:, h * DV_A:(h + 1) * DV_A]
        v_ref[pl.ds(h, tm, stride=H_A), :] = v[:, h * DV_A:(h + 1) * DV_A]
    kb_ref[...] = kn.astype(BF16)
    vb_ref[...] = v.astype(BF16)
    u_ref[...] = u
    g_ref[...] = g
    kk_ref[...] = kk
    vc_ref[...] = vc
    qh_ref[...] = qh
    sg_ref[...] = sg


def _inproj_decode_kernel(x_ref, lng_ref, w_ref, qg_ref, kg_ref, lb_ref, bd_ref,
                          q_ref, k_ref, v_ref, u_ref, gt_ref, kkt_ref, vc_ref, qht_ref, sg_ref):
    qn, kn, v, u, g, kk, vc, qh, sg = _inproj_body(x_ref, lng_ref, w_ref, qg_ref, kg_ref, lb_ref, bd_ref)
    q_ref[...] = qn
    k_ref[...] = kn
    v_ref[...] = v
    u_ref[...] = u
    gt_ref[...] = g.T
    kkt_ref[...] = kk.T
    vc_ref[...] = vc
    qht_ref[...] = qh.T
    sg_ref[...] = sg


def _inproj_in_specs(tm):
    row = lambda i: (i, 0)
    fix = lambda i: (0, 0)
    return [
        pl.BlockSpec((tm, D_MODEL), row),
        pl.BlockSpec((1, D_MODEL), fix),
        pl.BlockSpec((D_MODEL, D_IN), fix),
        pl.BlockSpec((1, QK_A), fix),
        pl.BlockSpec((1, QK_A), fix),
        pl.BlockSpec((1, W_C), fix),
        pl.BlockSpec((QK_A, QK_A), fix),
    ]


def _inproj_prompt(x, lng, w, qg, kg, lb, bd, k_all, v_all, layer, tm):
    t = x.shape[0]
    row = lambda i: (i, 0)

    def plain(wd, dt):
        return pl.BlockSpec((tm, wd), row), jax.ShapeDtypeStruct((t, wd), dt)

    def stacked(arr):
        return (pl.BlockSpec((None, tm * H_A, DV_A), lambda i: (layer, i, 0)),
                jax.ShapeDtypeStruct(arr.shape, arr.dtype))

    outs = [plain(QK_A, BF16), stacked(k_all), plain(QK_A, BF16), stacked(v_all), plain(W_A, BF16),
            plain(W_B, F32), plain(W_C, F32), plain(W_C, F32), plain(W_C, F32), plain(W_C, F32),
            plain(D_MODEL, F32)]
    n_in = len(_inproj_in_specs(tm))
    any_spec = pl.BlockSpec(memory_space=pl.ANY)
    return pl.pallas_call(
        _inproj_prompt_kernel,
        grid=(t // tm,),
        in_specs=_inproj_in_specs(tm) + [any_spec, any_spec],
        out_specs=[o[0] for o in outs],
        out_shape=[o[1] for o in outs],
        input_output_aliases={n_in: 1, n_in + 1: 3},
        name="inproj_prompt",
        compiler_params=pltpu.CompilerParams(
            dimension_semantics=("parallel",), vmem_limit_bytes=VMEM_LIMIT),
    )(x, lng, w, qg, kg, lb, bd, k_all, v_all)


def _inproj_decode(x, lng, w, qg, kg, lb, bd):
    t = x.shape[0]
    row = lambda i: (i, 0)
    shapes = [(t, QK_A), (t, QK_A), (t, W_A), (t, W_B), (W_C, t), (W_C, t), (t, W_C), (W_C, t),
              (t, D_MODEL)]
    return pl.pallas_call(
        _inproj_decode_kernel,
        grid=(1,),
        in_specs=_inproj_in_specs(t),
        out_specs=[pl.BlockSpec(s, row) for s in shapes],
        out_shape=[jax.ShapeDtypeStruct(s, F32) for s in shapes],
        name="inproj_decode",
        compiler_params=pltpu.CompilerParams(
            dimension_semantics=("arbitrary",), vmem_limit_bytes=VMEM_LIMIT),
    )(x, lng, w, qg, kg, lb, bd)


def _attn_kernel(par_ref, slope_ref, q_ref, k_ref, v_ref, kf_ref, sub_ref, o_ref, m_sc, l_sc, acc_sc,
                 *, blk, post_scale):
    del kf_ref
    hd = pl.program_id(1)
    qi = pl.program_id(2)
    slope = slope_ref[hd]
    lam = par_ref[0]
    q = q_ref[...]
    lane = lax.broadcasted_iota(jnp.int32, q.shape, 1)
    zero = jnp.zeros_like(q)
    qm = (jnp.where(lane < DK_A, q, zero), jnp.where(lane >= DK_A, q, zero))
    rel = (lax.broadcasted_iota(jnp.int32, (blk, blk), 0)
           - lax.broadcasted_iota(jnp.int32, (blk, blk), 1))
    srel = slope * rel.astype(F32)
    m_sc[...] = jnp.full(m_sc.shape, NEG, F32)
    l_sc[...] = jnp.zeros(l_sc.shape, F32)
    acc_sc[...] = jnp.zeros(acc_sc.shape, F32)

    def step(kj, masked):
        start = pl.multiple_of(kj * blk, blk)
        kb = k_ref[pl.ds(start, blk), :]
        vb = v_ref[pl.ds(start, blk), :]
        off = slope * ((qi - kj) * blk).astype(F32)
        for c in range(2):
            s = _dot_nt(qm[c], kb) - srel - off
            if masked:
                s = jnp.where(rel >= 0, s, NEG)
            m_old = m_sc[c]
            m_new = jnp.maximum(m_old, jnp.max(s, axis=-1, keepdims=True))
            alpha = jnp.exp(m_old - m_new)
            p = jnp.exp(s - m_new)
            l_sc[c] = alpha * l_sc[c] + jnp.sum(p, axis=-1, keepdims=True)
            acc_sc[c] = alpha * acc_sc[c] + _dot(p.astype(BF16), vb)
            m_sc[c] = m_new

    def body(kj, carry):
        step(kj, False)
        return carry

    lax.fori_loop(0, qi, body, 0)
    step(qi, True)
    o = acc_sc[0] / l_sc[0] - lam * (acc_sc[1] / l_sc[1])
    on = o * lax.rsqrt(jnp.mean(o * o, axis=-1, keepdims=True) + EPS) * sub_ref[...]
    o_ref[...] = on * post_scale


N_SHIFT_PARTS = 3
FEAT_SUB = 64


def _attn_k_features(blk):
    j = np.arange(blk)
    f = np.zeros((blk, DV_A), np.float32)
    f[:, 0] = 1.0
    f[:, 1] = 1.0
    f[:, 2] = j // FEAT_SUB
    f[:, 3] = j % FEAT_SUB
    f[:, 4:4 + N_SHIFT_PARTS] = 1.0
    return jnp.asarray(f, dtype=BF16)


def _attn_shift_kernel(par_ref, slope_ref, q_ref, k_ref, v_ref, kf_ref, sub_ref, o_ref, acc_sc,
                       *, blk, post_scale):
    hd = pl.program_id(1)
    qi = pl.program_id(2)
    slope = slope_ref[hd]
    lam = par_ref[0]
    q = q_ref[...]
    lane = lax.broadcasted_iota(jnp.int32, q.shape, 1)
    row = lax.broadcasted_iota(jnp.int32, q.shape, 0)
    ih = (row // FEAT_SUB).astype(F32)
    il = (row % FEAT_SUB).astype(F32)
    qf = jnp.where(lane == 0, -(slope * FEAT_SUB) * ih, 0.0)
    qf = jnp.where(lane == 1, -slope * il, qf)
    qf = jnp.where(lane == 2, slope * FEAT_SUB, qf)
    qf = jnp.where(lane == 3, slope, qf)
    for part in range(N_SHIFT_PARTS):
        qf = jnp.where(lane == 4 + part, -par_ref[1 + part], qf)
    qfb = qf.astype(BF16)
    zero = jnp.zeros_like(q)
    qe = (jnp.concatenate([jnp.where(lane < DK_A, q, zero), qfb], axis=1),
          jnp.concatenate([jnp.where(lane >= DK_A, q, zero), qfb], axis=1))
    kf = kf_ref[...]
    ones = jnp.ones((blk, DV_A), BF16)

    def step(kj, diagonal):
        start = pl.multiple_of(kj * blk, blk)
        kb = jnp.concatenate([k_ref[pl.ds(start, blk), :], kf], axis=1)
        vb = jnp.concatenate([v_ref[pl.ds(start, blk), :], ones], axis=1)
        cstep = slope * ((qi - kj) * blk).astype(F32)
        for c in range(2):
            p = jnp.exp(_dot_nt(qe[c], kb) - cstep)
            if diagonal:
                rel = (lax.broadcasted_iota(jnp.int32, (blk, blk), 0)
                       - lax.broadcasted_iota(jnp.int32, (blk, blk), 1))
                pv = _dot(jnp.where(rel >= 0, p, 0.0).astype(BF16), vb)
                acc_sc[c] = pv
            else:
                acc_sc[c] += _dot(p.astype(BF16), vb)

    step(qi, True)

    @pl.when(qi % 2 == 1)
    def _():
        step(qi - 1, False)

    def pair(i, carry):
        step(2 * i, False)
        step(2 * i + 1, False)
        return carry

    lax.fori_loop(0, qi // 2, pair, 0)
    a0 = acc_sc[0]
    a1 = acc_sc[1]
    o = a0[:, 0:DV_A] / a0[:, DV_A:] - lam * (a1[:, 0:DV_A] / a1[:, DV_A:])
    on = o * lax.rsqrt(jnp.mean(o * o, axis=-1, keepdims=True) + EPS) * sub_ref[...]
    o_ref[...] = on * post_scale


def _attn_prompt(par, slopes, q, k, v, kf, sub, nb, seq, blk, post_scale, shifted):
    t = nb * seq
    nq = seq // blk
    smem = pl.BlockSpec(memory_space=pltpu.SMEM)
    if shifted:
        body = functools.partial(_attn_shift_kernel, blk=blk, post_scale=post_scale)
        scratch = [pltpu.VMEM((2, blk, 2 * DV_A), F32)]
        name = "attn_prompt_shift"
    else:
        body = functools.partial(_attn_kernel, blk=blk, post_scale=post_scale)
        scratch = [pltpu.VMEM((2, blk, 1), F32), pltpu.VMEM((2, blk, 1), F32),
                   pltpu.VMEM((2, blk, DV_A), F32)]
        name = "attn_prompt_online"
    return pl.pallas_call(
        body,
        grid=(nb, H_A, nq),
        in_specs=[
            smem, smem,
            pl.BlockSpec((blk, DV_A), lambda b, h, i: (b * nq + i, h)),
            pl.BlockSpec((seq, DV_A), lambda b, h, i: (b, h)),
            pl.BlockSpec((seq, DV_A), lambda b, h, i: (b, h)),
            pl.BlockSpec((blk, DV_A), lambda b, h, i: (0, 0)),
            pl.BlockSpec((1, DV_A), lambda b, h, i: (0, 0)),
        ],
        out_specs=pl.BlockSpec((blk, DV_A), lambda b, h, i: (b * nq + i, h)),
        out_shape=jax.ShapeDtypeStruct((t, W_A), F32),
        scratch_shapes=scratch,
        name=name,
        compiler_params=pltpu.CompilerParams(
            dimension_semantics=("parallel", "parallel", "arbitrary"),
            vmem_limit_bytes=VMEM_LIMIT),
    )(par, slopes, q, k, v, kf, sub)


def _block_diag_rows(m):
    head = lax.broadcasted_iota(jnp.int32, m.shape, 1) // DK_C
    parts = [jnp.where(head == hp, m, 0.0).astype(BF16) for hp in range(H_C)]
    return jnp.concatenate(parts, axis=0)


def _mix_kernel(x_ref, oa_ref, u_ref, g_ref, kk_ref, vc_ref, qh_ref, sg_ref,
                ltri_ref, hh_ref, pw_ref, ps_ref, hg_ref, wo_ref,
                xo_ref, pool_ref, hst_ref,
                ubuf, kkbuf, gbuf, vcbuf, st_sc, oc_sc, *, tc):
    si = pl.program_id(1)
    ns = pl.num_programs(1)
    pad = HGRN_SUB
    nchunk = tc // HGRN_CHUNK

    @pl.when(si == 0)
    def _():
        st_sc[...] = jnp.zeros(st_sc.shape, F32)
        z = jnp.zeros((pad, W_B), F32)
        ubuf[0:pad, :] = z
        kkbuf[0:pad, :] = z
        gbuf[0:pad, :] = z
        vcbuf[0:pad, :] = z
        oc_sc[...] = jnp.zeros(oc_sc.shape, F32)

    u = u_ref[...]
    ubuf[pad:pad + tc, :] = u
    acc = ubuf[...]
    wsum = {}
    for k, win in enumerate(POOL_WINDOWS):
        acc = acc + pltpu.roll(acc, 2 ** k, 0)
        wsum[win] = acc[pad:, :]
    pos = (si * tc + lax.broadcasted_iota(jnp.int32, (tc, 1), 0)).astype(F32)
    lane_b = lax.broadcasted_iota(jnp.int32, (tc, W_B), 1)
    pooled = None
    for gi, win in reversed(list(enumerate(POOL_WINDOWS))):
        term = wsum[win] * (1.0 / jnp.minimum(pos + 1.0, float(win)))
        pooled = term if pooled is None else jnp.where(lane_b < (gi + 1) * C_B, term, pooled)
    pooled = pooled - u
    ob = _dot(pooled.astype(BF16), pw_ref[...]) * ps_ref[...]
    ubuf[0:pad, :] = ubuf[tc:tc + pad, :]

    xo_ref[...] = (x_ref[...]
                   + _dot((oa_ref[...] * sg_ref[:, 0:W_A]).astype(BF16), wo_ref[0:W_A, :])
                   + _dot((ob * sg_ref[:, W_A:W_A + W_B]).astype(BF16), wo_ref[W_A:W_A + W_B, :]))

    g = g_ref[...]
    kk = kk_ref[...]
    vc = vc_ref[...]
    qh = qh_ref[...]
    ltri = ltri_ref[...]
    ghi, gmid, glo = _split3(g)
    gc = _dot(ltri, ghi) + _dot(ltri, gmid) + _dot(ltri, glo)
    gc3 = gc.reshape(nchunk, HGRN_CHUNK, W_C)

    def chunk_row(idx):
        r = jnp.broadcast_to(gc3[:, idx:idx + 1, :], (nchunk, HGRN_CHUNK, W_C))
        return r.reshape(tc, W_C)

    rc = lax.broadcasted_iota(jnp.int32, (tc, 1), 0) % HGRN_CHUNK
    hh = hh_ref[...]
    t_idx = lax.broadcasted_iota(jnp.int32, (HGRN_CHUNK, H_C * HGRN_CHUNK), 0)
    s_idx = lax.broadcasted_iota(jnp.int32, (HGRN_CHUNK, H_C * HGRN_CHUNK), 1) % HGRN_CHUNK
    chunks = [slice(c * HGRN_CHUNK, (c + 1) * HGRN_CHUNK) for c in range(nchunk)]

    rmid = chunk_row(HGRN_CHUNK // 2 - 1)
    safe = jnp.max(jnp.abs(gc - rmid)) <= HGRN_SAFE_EXP

    @pl.when(jnp.logical_not(safe))
    def _():
        upper = rc >= 32
        r31 = chunk_row(31)
        qt1 = jnp.where(upper, qh * jnp.exp(jnp.minimum(gc - r31, 0.0)), 0.0)
        kt1 = jnp.where(upper, 0.0, kk * jnp.exp(jnp.minimum(r31 - gc, 0.0)))
        ref2 = jnp.where(upper, chunk_row(47), chunk_row(15))
        odd = ((rc // HGRN_SUB) % 2) == 1
        qt2 = jnp.where(odd, qh * jnp.exp(jnp.minimum(gc - ref2, 0.0)), 0.0)
        kt2 = jnp.where(odd, 0.0, kk * jnp.exp(jnp.minimum(ref2 - gc, 0.0)))
        tb = t_idx // HGRN_SUB
        mask2 = ((tb % 2) == 1) & (s_idx // HGRN_SUB == tb - 1)

        kkbuf[pad:pad + tc, :] = kk
        gbuf[pad:pad + tc, :] = gc
        vcbuf[pad:pad + tc, :] = vc
        r16 = rc % HGRN_SUB
        od = jnp.zeros((tc, W_C), F32)
        for d in range(HGRN_SUB):
            kks = kkbuf[pad - d:pad - d + tc, :]
            gs = gbuf[pad - d:pad - d + tc, :]
            vcs = vcbuf[pad - d:pad - d + tc, :]
            xd = jnp.where(r16 >= d, qh * kks * jnp.exp(jnp.minimum(gc - gs, 0.0)), 0.0)
            od = od + _dot(xd.astype(BF16), hh) * vcs
        for sl in chunks:
            a1 = _dot_nt(qt1[sl].astype(BF16), _block_diag_rows(kt1[sl]))
            a2 = _dot_nt(qt2[sl].astype(BF16), _block_diag_rows(kt2[sl]))
            a_off = a1 + jnp.where(mask2, a2, 0.0)
            oc_sc[sl, :] = _dot(a_off.astype(BF16), _block_diag_rows(vc[sl])) + od[sl]

    eg = jnp.exp(gc)
    qe = qh * eg
    kdec = kk * jnp.exp(chunk_row(HGRN_CHUNK - 1) - gc)
    qt = qh * jnp.exp(gc - rmid)
    kt = kk * jnp.exp(rmid - gc)
    bdmask = (lax.broadcasted_iota(jnp.int32, (W_C, W_C), 0) // DK_C
              == lax.broadcasted_iota(jnp.int32, (W_C, W_C), 1) // DK_C)
    outs = []
    for c, sl in enumerate(chunks):
        a = _dot_nt(qt[sl].astype(BF16), _block_diag_rows(kt[sl]))
        a = jnp.where(t_idx >= s_idx, a, 0.0)
        intra = jnp.where(safe, _dot(a.astype(BF16), _block_diag_rows(vc[sl])), oc_sc[sl, :])
        st = st_sc[...]
        outs.append(intra + _dot_nt(qe[sl].astype(BF16), st.astype(BF16)))
        last = c * HGRN_CHUNK + HGRN_CHUNK - 1
        upd = _dot_tn(vc[sl].astype(BF16), kdec[sl].astype(BF16))
        st_sc[...] = st * eg[last:last + 1, :] + jnp.where(bdmask, upd, 0.0)

    o = jnp.concatenate(outs, axis=0)
    ss = _dot((o * o).astype(BF16), hh)
    ocn = o * lax.rsqrt(ss * (1.0 / DV_C) + EPS) * hg_ref[...]
    xo_ref[...] += _dot((ocn * sg_ref[:, W_A + W_B:]).astype(BF16), wo_ref[W_A + W_B:, :])

    @pl.when(si == ns - 1)
    def _():
        pool_ref[0] = u[tc - POOL_BUF:, :]
        hst_ref[0] = st_sc[...]


def _mix_prompt(x, oa, u, g, kk, vc, qh, sg, ltri, hh, pw, ps, hg, wo, nb, seq, tc):
    t = nb * seq
    ns = seq // tc
    row = lambda b, s: (b * ns + s, 0)
    fix = lambda b, s: (0, 0)
    in_specs = [pl.BlockSpec((tc, D_MODEL), row), pl.BlockSpec((tc, W_A), row)]
    in_specs += [pl.BlockSpec((tc, W_C), row) for _ in range(5)]
    in_specs += [
        pl.BlockSpec((tc, D_MODEL), row),
        pl.BlockSpec((tc, tc), fix),
        pl.BlockSpec((W_C, W_C), fix),
        pl.BlockSpec((W_B, W_B), fix),
        pl.BlockSpec((1, W_B), fix),
        pl.BlockSpec((1, W_C), fix),
        pl.BlockSpec((D_MODEL, D_MODEL), fix),
    ]
    return pl.pallas_call(
        functools.partial(_mix_kernel, tc=tc),
        grid=(nb, ns),
        in_specs=in_specs,
        out_specs=[
            pl.BlockSpec((tc, D_MODEL), row),
            pl.BlockSpec((1, POOL_BUF, W_B), lambda b, s: (b, 0, 0)),
            pl.BlockSpec((1, W_C, W_C), lambda b, s: (b, 0, 0)),
        ],
        out_shape=[
            jax.ShapeDtypeStruct((t, D_MODEL), F32),
            jax.ShapeDtypeStruct((nb, POOL_BUF, W_B), F32),
            jax.ShapeDtypeStruct((nb, W_C, W_C), F32),
        ],
        scratch_shapes=[
            pltpu.VMEM((tc + HGRN_SUB, W_B), F32),
            pltpu.VMEM((tc + HGRN_SUB, W_C), F32),
            pltpu.VMEM((tc + HGRN_SUB, W_C), F32),
            pltpu.VMEM((tc + HGRN_SUB, W_C), F32),
            pltpu.VMEM((W_C, W_C), F32),
            pltpu.VMEM((tc, W_C), F32),
        ],
        name="mix_prompt",
        compiler_params=pltpu.CompilerParams(
            dimension_semantics=("parallel", "arbitrary"), vmem_limit_bytes=VMEM_LIMIT),
    )(x, oa, u, g, kk, vc, qh, sg, ltri, hh, pw, ps, hg, wo)


ATT_ROWS = 16


def _paged_attn_kernel(pt_ref, lam_ref, slope_ref, q_ref, kn_ref, vn_ref, sub_ref, *rest,
                       n_pages, post_scale):
    del pt_ref
    k_refs = rest[:n_pages]
    v_refs = rest[n_pages:2 * n_pages]
    o_ref = rest[2 * n_pages]
    s_sc = rest[2 * n_pages + 1]
    rows = PAGE_SIZE * H_A
    past = n_pages * PAGE_SIZE
    lam = lam_ref[0]

    r = lax.broadcasted_iota(jnp.int32, (ATT_ROWS, DV_A), 0)
    lane = lax.broadcasted_iota(jnp.int32, (ATT_ROWS, DV_A), 1)

    def head_rows(row):
        out = jnp.zeros((ATT_ROWS, DV_A), F32)
        for h in range(H_A):
            piece = jnp.broadcast_to(row[:, h * DV_A:(h + 1) * DV_A], (ATT_ROWS, DV_A))
            out = jnp.where(r % H_A == h, piece, out)
        return out

    q = jnp.where(lane // DK_A == r // (ATT_ROWS // 2), head_rows(q_ref[0]), 0.0)
    kn = head_rows(kn_ref[0])
    vn = head_rows(vn_ref[0])
    qb = q.astype(BF16)
    slope = slope_ref[...]

    col = lax.broadcasted_iota(jnp.int32, (ATT_ROWS, rows), 1)
    rr = lax.broadcasted_iota(jnp.int32, (ATT_ROWS, rows), 0)
    valid = (col % H_A) == (rr % H_A)
    tok = col // H_A
    for j in range(n_pages):
        s = _dot_nt(qb, k_refs[j][...].astype(BF16))
        dist = (past - j * PAGE_SIZE - tok).astype(F32)
        s_sc[:, j * rows:(j + 1) * rows] = jnp.where(valid, s - slope * dist, NEG)
    s_self = jnp.sum(q * kn, axis=-1, keepdims=True)
    s_all = s_sc[...]
    m = jnp.maximum(jnp.max(s_all, axis=-1, keepdims=True), s_self)
    p_self = jnp.exp(s_self - m)
    l = p_self
    acc = p_self * vn
    for j in range(n_pages):
        p = jnp.exp(s_all[:, j * rows:(j + 1) * rows] - m)
        l = l + jnp.sum(p, axis=-1, keepdims=True)
        acc = acc + _dot(p.astype(BF16), v_refs[j][...].astype(BF16))
    o16 = acc / l
    half = ATT_ROWS // 2
    o = o16[0:half] - lam * o16[half:ATT_ROWS]
    on = o * lax.rsqrt(jnp.mean(o * o, axis=-1, keepdims=True) + EPS) * sub_ref[...]
    o_ref[0] = (on * post_scale)[0:H_A]


def _attn_paged(pt_flat, lam, slope_rows, q, kn, vn, sub, ck, cv, layer, n_pages, post_scale):
    nb = q.shape[0]
    rows = PAGE_SIZE * H_A
    smem = pl.BlockSpec(memory_space=pltpu.SMEM)
    row3 = lambda b, pt: (b, 0, 0)

    def page_spec(j):
        return pl.BlockSpec((None, None, rows, DV_A),
                            lambda b, pt, j=j: (layer, pt[b * n_pages + j], 0, 0))

    in_specs = [smem,
                pl.BlockSpec((ATT_ROWS, 1), lambda b, pt: (0, 0)),
                pl.BlockSpec((1, 1, QK_A), row3),
                pl.BlockSpec((1, 1, QK_A), row3),
                pl.BlockSpec((1, 1, W_A), row3),
                pl.BlockSpec((1, DV_A), lambda b, pt: (0, 0))]
    in_specs += [page_spec(j) for j in range(n_pages)] * 2
    grid_spec = pltpu.PrefetchScalarGridSpec(
        num_scalar_prefetch=1,
        grid=(nb,),
        in_specs=in_specs,
        out_specs=pl.BlockSpec((1, H_A, DV_A), row3),
        scratch_shapes=[pltpu.VMEM((ATT_ROWS, n_pages * rows), F32)],
    )
    return pl.pallas_call(
        functools.partial(_paged_attn_kernel, n_pages=n_pages, post_scale=post_scale),
        grid_spec=grid_spec,
        out_shape=jax.ShapeDtypeStruct((nb, H_A, DV_A), F32),
        name="attn_paged",
        compiler_params=pltpu.CompilerParams(
            dimension_semantics=("arbitrary",), vmem_limit_bytes=VMEM_LIMIT),
    )(pt_flat, lam, slope_rows, q.reshape(nb, 1, QK_A), kn.reshape(nb, 1, QK_A),
      vn.reshape(nb, 1, W_A), sub, *([ck] * n_pages), *([cv] * n_pages))


def _dec_mix_kernel(x_ref, oa_ref, u_ref, gt_ref, kkt_ref, qht_ref, vc_ref, sg_ref, sp_ref, sh_ref,
                    pw_ref, ps_ref, hg_ref, wo_ref,
                    xo_ref, po_ref, ho_ref, pooled_sc, o_sc, *, nbt, past):
    i = pl.program_id(0)
    ridx = lax.broadcasted_iota(jnp.int32, (POOL_BUF, W_B), 0)
    lane_b = lax.broadcasted_iota(jnp.int32, (1, W_B), 1)
    lane_t = lax.broadcasted_iota(jnp.int32, gt_ref.shape, 1)

    def per_batch(b, carry):
        prev = sp_ref[b]
        urow = u_ref[pl.ds(b, 1), :]
        pooled = None
        for gi, win in reversed(list(enumerate(POOL_WINDOWS))):
            ws = urow + jnp.sum(jnp.where(ridx >= POOL_BUF + 1 - win, prev, 0.0), axis=0, keepdims=True)
            term = ws * (1.0 / min(past + 1, win))
            pooled = term if pooled is None else jnp.where(lane_b < (gi + 1) * C_B, term, pooled)
        pooled_sc[pl.ds(b, 1), :] = pooled - urow
        po_ref[b, 0:POOL_BUF - 1, :] = prev[1:POOL_BUF, :]
        po_ref[b, POOL_BUF - 1:POOL_BUF, :] = urow

        bl = i * nbt + b

        def col(ref):
            return jnp.sum(jnp.where(lane_t == bl, ref[...], 0.0), axis=1, keepdims=True)

        fcol = jnp.exp(col(gt_ref))
        kcol = col(kkt_ref)
        qcol = col(qht_ref)
        vrow = vc_ref[pl.ds(b, 1), :]
        vrows = jnp.concatenate(
            [jnp.broadcast_to(vrow[:, h * DV_C:(h + 1) * DV_C], (DK_C, DV_C)) for h in range(H_C)],
            axis=0)
        s_new = fcol * sh_ref[b] + kcol * vrows
        ho_ref[b] = s_new
        o4 = jnp.sum((s_new * qcol).reshape(H_C, DK_C, DV_C), axis=1)
        for h in range(H_C):
            o_sc[h, pl.ds(b, 1), :] = o4[h:h + 1, :]
        return carry

    lax.fori_loop(0, nbt, per_batch, 0)

    ob = _dot(pooled_sc[...].astype(BF16), pw_ref[...]) * ps_ref[...]
    sg = sg_ref[...]
    y = (_dot((oa_ref[...] * sg[:, 0:W_A]).astype(BF16), wo_ref[0:W_A, :])
         + _dot((ob * sg[:, W_A:W_A + W_B]).astype(BF16), wo_ref[W_A:W_A + W_B, :]))
    for h in range(H_C):
        oh = o_sc[h]
        ohn = oh * lax.rsqrt(jnp.mean(oh * oh, axis=-1, keepdims=True) + EPS) * hg_ref[...]
        lo = W_A + W_B + h * DV_C
        y = y + _dot((ohn * sg[:, lo:lo + DV_C]).astype(BF16), wo_ref[lo:lo + DV_C, :])
    xo_ref[...] = x_ref[...] + y


def _mix_decode(x, oa, u, gt, kkt, qht, vc, sg, sp, sh, pw, ps, hg, wo, nbt, past):
    nb = x.shape[0]
    row = lambda i: (i, 0)
    fix = lambda i: (0, 0)
    row3 = lambda i: (i, 0, 0)
    in_specs = [
        pl.BlockSpec((nbt, D_MODEL), row),
        pl.BlockSpec((nbt, W_A), row),
        pl.BlockSpec((nbt, W_B), row),
        pl.BlockSpec((W_C, nb), fix),
        pl.BlockSpec((W_C, nb), fix),
        pl.BlockSpec((W_C, nb), fix),
        pl.BlockSpec((nbt, W_C), row),
        pl.BlockSpec((nbt, D_MODEL), row),
        pl.BlockSpec((nbt, POOL_BUF, W_B), row3),
        pl.BlockSpec((nbt, W_C, DV_C), row3),
        pl.BlockSpec((W_B, W_B), fix),
        pl.BlockSpec((1, W_B), fix),
        pl.BlockSpec((1, DV_C), fix),
        pl.BlockSpec((D_MODEL, D_MODEL), fix),
    ]
    return pl.pallas_call(
        functools.partial(_dec_mix_kernel, nbt=nbt, past=past),
        grid=(nb // nbt,),
        in_specs=in_specs,
        out_specs=[
            pl.BlockSpec((nbt, D_MODEL), row),
            pl.BlockSpec((nbt, POOL_BUF, W_B), row3),
            pl.BlockSpec((nbt, W_C, DV_C), row3),
        ],
        out_shape=[
            jax.ShapeDtypeStruct((nb, D_MODEL), F32),
            jax.ShapeDtypeStruct((nb, POOL_BUF, W_B), F32),
            jax.ShapeDtypeStruct((nb, W_C, DV_C), F32),
        ],
        scratch_shapes=[
            pltpu.VMEM((nbt, W_B), F32),
            pltpu.VMEM((H_C, nbt, DV_C), F32),
        ],
        name="mix_decode",
        compiler_params=pltpu.CompilerParams(
            dimension_semantics=("arbitrary",), vmem_limit_bytes=VMEM_LIMIT),
    )(x, oa, u, gt, kkt, qht, vc, sg, sp, sh, pw, ps, hg, wo)


def _block_ones(n, blk):
    idx = np.arange(n) // blk
    return jnp.asarray((idx[:, None] == idx[None, :]).astype(np.float32), dtype=BF16)


def _chunk_lower_tri(n, blk):
    idx = np.arange(n)
    same = (idx[:, None] // blk) == (idx[None, :] // blk)
    return jnp.asarray((same & (idx[:, None] >= idx[None, :])).astype(np.float32), dtype=BF16)


def kernel(x_prompt, x_sample, cache_k, cache_v, state_pool, state_hgrn, page_table,
           ln_gain, w_in, qn_gain, kn_gain, lam_q1, lam_k1, lam_q2, lam_k2, subln_gain,
           pool_w, pool_scale, hgrn_lb, hgrn_norm_gain, w_out):
    nb, seq, _ = x_prompt.shape
    t = nb * seq
    db = x_sample.shape[0]
    n_pages = page_table.shape[1]
    past = n_pages * PAGE_SIZE
    n_phys = cache_k.shape[1]

    tm = min(512, t)
    blk = min(512, seq)
    tc = min(256, seq)
    nbt = min(16, db)

    w_in_b = w_in.astype(BF16)
    w_out_b = w_out.astype(BF16)
    sm = jax.nn.softmax(hgrn_lb.astype(F32), axis=0)
    lb_all = jnp.cumsum(sm, axis=0) - sm[0]
    lam_init = [0.8 - 0.6 * math.exp(-0.3 * l) for l in range(DEPTH)]
    lam_all = (jnp.exp(jnp.sum(lam_q1.astype(F32) * lam_k1.astype(F32), axis=-1))
               - jnp.exp(jnp.sum(lam_q2.astype(F32) * lam_k2.astype(F32), axis=-1))
               + jnp.asarray(lam_init, F32))
    slopes_np = np.asarray([2.0 ** (-8.0 * (h + 1) / H_A) for h in range(H_A)], np.float32)
    slopes = jnp.asarray(slopes_np)
    slope_rows = jnp.asarray(slopes_np[np.arange(ATT_ROWS) % H_A].reshape(ATT_ROWS, 1))
    score_bound = (1.0125 * DK_A ** 0.5) * jnp.max(
        jnp.abs(qn_gain.astype(F32) * kn_gain.astype(F32)), axis=-1)
    sb_hi = score_bound.astype(BF16).astype(F32)
    sb_mid = (score_bound - sb_hi).astype(BF16).astype(F32)
    sb_lo = (score_bound - sb_hi - sb_mid).astype(BF16).astype(F32)
    attn_par = jnp.stack([lam_all, sb_hi, sb_mid, sb_lo], axis=1)
    kfeat = _attn_k_features(blk)
    qg = jnp.tile(qn_gain.astype(F32), (1, QK_A // DK_A))
    kg = jnp.tile(kn_gain.astype(F32), (1, QK_A // DK_A))
    hg_tiled = jnp.tile(hgrn_norm_gain.astype(F32), (1, H_C))
    eye_g = jnp.eye(G_B, dtype=F32)
    pw_bd = jnp.einsum('lgcd,gh->lgchd', pool_w.astype(F32), eye_g).reshape(DEPTH, W_B, W_B).astype(BF16)
    bd_qk = _block_ones(QK_A, DK_A)
    hh = _block_ones(W_C, DV_C)
    ltri = _chunk_lower_tri(tc, HGRN_CHUNK)
    ck = cache_k.reshape(DEPTH, n_phys, PAGE_SIZE * H_A, 2 * DK_A)
    cv = cache_v.reshape(DEPTH, n_phys, PAGE_SIZE * H_A, DV_A)
    pt_flat = page_table.reshape(-1).astype(jnp.int32)

    xp = x_prompt.reshape(t, D_MODEL)
    xs = x_sample.reshape(db, D_MODEL)
    k_all = jnp.zeros((DEPTH, t * H_A, 2 * DK_A), F32)
    v_all = jnp.zeros((DEPTH, t * H_A, DV_A), F32)
    pools, states = [], []
    dks, dvs, dpools, dstates = [], [], [], []
    for l in range(DEPTH):
        lng = ln_gain[l].reshape(1, D_MODEL).astype(F32)
        lb = lb_all[l].reshape(1, W_C)
        lam = lam_all[l].reshape(1)
        sub = subln_gain[l].reshape(1, DV_A).astype(F32)
        ps = pool_scale[l].reshape(1, W_B).astype(F32)
        post = 1.0 - lam_init[l]

        q, k_all, kb, v_all, vb, u, g, kk, vc, qh, sg = _inproj_prompt(
            xp, lng, w_in_b[l], qg[l:l + 1], kg[l:l + 1], lb, bd_qk, k_all, v_all, l, tm)
        oa = lax.cond(
            score_bound[l] <= ATTN_SAFE_SHIFT,
            functools.partial(_attn_prompt, nb=nb, seq=seq, blk=blk, post_scale=post, shifted=True),
            functools.partial(_attn_prompt, nb=nb, seq=seq, blk=blk, post_scale=post, shifted=False),
            attn_par[l], slopes, q, kb, vb, kfeat, sub)
        xp, pool_new, st = _mix_prompt(xp, oa, u, g, kk, vc, qh, sg, ltri, hh, pw_bd[l], ps,
                                       hg_tiled[l:l + 1], w_out_b[l], nb, seq, tc)
        pools.append(pool_new)
        states.append(st)

        dq, dk, dv, du, dgt, dkkt, dvc, dqht, dsg = _inproj_decode(
            xs, lng, w_in_b[l], qg[l:l + 1], kg[l:l + 1], lb, bd_qk)
        doa = _attn_paged(pt_flat, lam, slope_rows, dq, dk, dv, sub, ck, cv, l, n_pages, post)
        xs, dpool, dstate = _mix_decode(
            xs, doa.reshape(db, W_A), du, dgt, dkkt, dqht, dvc, dsg, state_pool[l],
            state_hgrn[l].reshape(db, W_C, DV_C), pw_bd[l], ps,
            hgrn_norm_gain[l].reshape(1, DV_C).astype(F32), w_out_b[l], nbt, past)
        dks.append(dk)
        dvs.append(dv)
        dpools.append(dpool)
        dstates.append(dstate)

    y_prompt = xp.reshape(nb, seq, D_MODEL)
    y_sample = xs.reshape(db, 1, D_MODEL)
    k_prompt = k_all.reshape(DEPTH, nb, seq, H_A, 2 * DK_A)
    v_prompt = v_all.reshape(DEPTH, nb, seq, H_A, DV_A)
    k_sample = jnp.stack(dks).reshape(DEPTH, db, 1, H_A, 2 * DK_A)
    v_sample = jnp.stack(dvs).reshape(DEPTH, db, 1, H_A, DV_A)
    pool_prompt = jnp.stack(pools)
    pool_sample = jnp.stack(dpools)
    st_all = jnp.stack(states).reshape(DEPTH, nb, H_C, DV_C, H_C, DK_C)
    hgrn_prompt = jnp.stack([st_all[:, :, h, :, h, :] for h in range(H_C)], axis=2).swapaxes(-1, -2)
    hgrn_sample = jnp.stack(dstates).reshape(DEPTH, db, H_C, DK_C, DV_C)
    return (y_prompt, y_sample, k_prompt, v_prompt, k_sample, v_sample,
            pool_prompt, pool_sample, hgrn_prompt, hgrn_sample)
```

```python
import functools
import math

import numpy as np
import jax
import jax.numpy as jnp
from jax import lax
from jax.experimental import pallas as pl
from jax.experimental.pallas import tpu as pltpu

F32 = jnp.float32
BF16 = jnp.bfloat16

D_MODEL = 1024
DEPTH = 4
PAGE_SIZE = 128
H_A = 4
DK_A = 64
DV_A = 128
W_A = H_A * DV_A
QK_A = H_A * 2 * DK_A
G_B = 4
POOL_WINDOWS = (2, 4, 8, 16)
W_B = 256
C_B = W_B // G_B
POOL_BUF = max(POOL_WINDOWS) - 1
H_C = 4
W_C = 256
DK_C = 64
DV_C = 64
D_IN = 2 * QK_A + W_A + W_B + 3 * W_C + D_MODEL
HGRN_CHUNK = 64
HGRN_SUB = 16
HGRN_SAFE_EXP = 60.0
ATTN_SAFE_SHIFT = 20.0
EPS = 1e-6
NEG = -1e30
assert POOL_WINDOWS == tuple(2 ** (k + 1) for k in range(len(POOL_WINDOWS)))

C_Q, C_K, C_V, C_U = 0, QK_A, 2 * QK_A, 2 * QK_A + W_A
C_F = C_U + W_B
C_I = C_F + W_C
C_QC = C_I + W_C
C_GATE = C_QC + W_C

V7X_VMEM_BYTES = 64 * 1024 * 1024
VMEM_LIMIT = 56 * 1024 * 1024


def _dot(a, b):
    return jnp.dot(a, b, preferred_element_type=F32)


def _dot_nt(a, b):
    return lax.dot_general(a, b, (((1,), (1,)), ((), ())), preferred_element_type=F32)


def _dot_tn(a, b):
    return lax.dot_general(a, b, (((0,), (0,)), ((), ())), preferred_element_type=F32)


def _split3(x):
    hi = x.astype(BF16)
    r = x - hi.astype(F32)
    mid = r.astype(BF16)
    lo = (r - mid.astype(F32)).astype(BF16)
    return hi, mid, lo


def _sigmoid(x):
    return 1.0 / (1.0 + jnp.exp(-x))


def _inproj_body(x_ref, lng_ref, w_ref, qg_ref, kg_ref, lb_ref, bd_ref):
    x = x_ref[...]
    h = x * lax.rsqrt(jnp.mean(x * x, axis=-1, keepdims=True) + EPS) * lng_ref[...]
    hb = h.astype(BF16)
    bd = bd_ref[...]

    def proj(lo, hi):
        return _dot(hb, w_ref[:, lo:hi])

    def head_norm(z, gain):
        ss = _dot((z * z).astype(BF16), bd)
        return z * lax.rsqrt(ss * (1.0 / DK_A) + EPS) * gain

    qn = head_norm(proj(C_Q, C_K), qg_ref[...]) * (DK_A ** -0.5)
    kn = head_norm(proj(C_K, C_V), kg_ref[...])
    v = proj(C_V, C_U)
    u = proj(C_U, C_F)
    lb = lb_ref[...]
    sig = _sigmoid(proj(C_F, C_I))
    g = jnp.log(lb + (1.0 - lb) * sig)
    kk = (1.0 - lb) * (1.0 - sig)
    vc = proj(C_I, C_QC)
    qc = proj(C_QC, C_GATE)
    qh = qc * _sigmoid(qc)
    gate = proj(C_GATE, D_IN)
    sg = gate * _sigmoid(gate)
    return qn, kn, v, u, g, kk, vc, qh, sg


def _inproj_prompt_kernel(x_ref, lng_ref, w_ref, qg_ref, kg_ref, lb_ref, bd_ref, kall_ref, vall_ref,
                          q_ref, k_ref, kb_ref, v_ref, vb_ref, u_ref, g_ref, kk_ref, vc_ref,
                          qh_ref, sg_ref):
    del kall_ref, vall_ref
    qn, kn, v, u, g, kk, vc, qh, sg = _inproj_body(x_ref, lng_ref, w_ref, qg_ref, kg_ref, lb_ref, bd_ref)
    tm = x_ref.shape[0]
    q_ref[...] = qn.astype(BF16)
    for h in range(H_A):
        k_ref[pl.ds(h, tm, stride=H_A), :] = kn[:, h * DV_A:(h + 1) * DV_A]
        v_ref[pl.ds(h, tm, stride=H_A), :] = v[:, h * DV_A:(h + 1) * DV_A]
    kb_ref[...] = kn.astype(BF16)
    vb_ref[...] = v.astype(BF16)
    u_ref[...] = u
    g_ref[...] = g
    kk_ref[...] = kk
    vc_ref[...] = vc
    qh_ref[...] = qh
    sg_ref[...] = sg


def _inproj_decode_kernel(x_ref, lng_ref, w_ref, qg_ref, kg_ref, lb_ref, bd_ref,
                          q_ref, k_ref, v_ref, u_ref, gt_ref, kkt_ref, vc_ref, qht_ref, sg_ref):
    qn, kn, v, u, g, kk, vc, qh, sg = _inproj_body(x_ref, lng_ref, w_ref, qg_ref, kg_ref, lb_ref, bd_ref)
    q_ref[...] = qn
    k_ref[...] = kn
    v_ref[...] = v
    u_ref[...] = u
    gt_ref[...] = g.T
    kkt_ref[...] = kk.T
    vc_ref[...] = vc
    qht_ref[...] = qh.T
    sg_ref[...] = sg


def _inproj_in_specs(tm):
    row = lambda i: (i, 0)
    fix = lambda i: (0, 0)
    return [
        pl.BlockSpec((tm, D_MODEL), row),
        pl.BlockSpec((1, D_MODEL), fix),
        pl.BlockSpec((D_MODEL, D_IN), fix),
        pl.BlockSpec((1, QK_A), fix),
        pl.BlockSpec((1, QK_A), fix),
        pl.BlockSpec((1, W_C), fix),
        pl.BlockSpec((QK_A, QK_A), fix),
    ]


def _inproj_prompt(x, lng, w, qg, kg, lb, bd, k_all, v_all, layer, tm):
    t = x.shape[0]
    row = lambda i: (i, 0)

    def plain(wd, dt):
        return pl.BlockSpec((tm, wd), row), jax.ShapeDtypeStruct((t, wd), dt)

    def stacked(arr):
        return (pl.BlockSpec((None, tm * H_A, DV_A), lambda i: (layer, i, 0)),
                jax.ShapeDtypeStruct(arr.shape, arr.dtype))

    outs = [plain(QK_A, BF16), stacked(k_all), plain(QK_A, BF16), stacked(v_all), plain(W_A, BF16),
            plain(W_B, F32), plain(W_C, F32), plain(W_C, F32), plain(W_C, F32), plain(W_C, F32),
            plain(D_MODEL, F32)]
    n_in = len(_inproj_in_specs(tm))
    any_spec = pl.BlockSpec(memory_space=pl.ANY)
    return pl.pallas_call(
        _inproj_prompt_kernel,
        grid=(t // tm,),
        in_specs=_inproj_in_specs(tm) + [any_spec, any_spec],
        out_specs=[o[0] for o in outs],
        out_shape=[o[1] for o in outs],
        input_output_aliases={n_in: 1, n_in + 1: 3},
        name="inproj_prompt",
        compiler_params=pltpu.CompilerParams(
            dimension_semantics=("parallel",), vmem_limit_bytes=VMEM_LIMIT),
    )(x, lng, w, qg, kg, lb, bd, k_all, v_all)


def _inproj_decode(x, lng, w, qg, kg, lb, bd):
    t = x.shape[0]
    row = lambda i: (i, 0)
    shapes = [(t, QK_A), (t, QK_A), (t, W_A), (t, W_B), (W_C, t), (W_C, t), (t, W_C), (W_C, t),
              (t, D_MODEL)]
    return pl.pallas_call(
        _inproj_decode_kernel,
        grid=(1,),
        in_specs=_inproj_in_specs(t),
        out_specs=[pl.BlockSpec(s, row) for s in shapes],
        out_shape=[jax.ShapeDtypeStruct(s, F32) for s in shapes],
        name="inproj_decode",
        compiler_params=pltpu.CompilerParams(
            dimension_semantics=("arbitrary",), vmem_limit_bytes=VMEM_LIMIT),
    )(x, lng, w, qg, kg, lb, bd)


def _attn_kernel(par_ref, slope_ref, q_ref, qf_ref, k_ref, v_ref, kf_ref, sub_ref, o_ref,
                 m_sc, l_sc, acc_sc, *, blk, post_scale):
    del qf_ref, kf_ref
    hd = pl.program_id(1)
    qi = pl.program_id(2)
    slope = slope_ref[hd]
    lam = par_ref[0]
    q = q_ref[...]
    lane = lax.broadcasted_iota(jnp.int32, q.shape, 1)
    zero = jnp.zeros_like(q)
    qm = (jnp.where(lane < DK_A, q, zero), jnp.where(lane >= DK_A, q, zero))
    rel = (lax.broadcasted_iota(jnp.int32, (blk, blk), 0)
           - lax.broadcasted_iota(jnp.int32, (blk, blk), 1))
    srel = slope * rel.astype(F32)
    m_sc[...] = jnp.full(m_sc.shape, NEG, F32)
    l_sc[...] = jnp.zeros(l_sc.shape, F32)
    acc_sc[...] = jnp.zeros(acc_sc.shape, F32)

    def step(kj, masked):
        start = pl.multiple_of(kj * blk, blk)
        kb = k_ref[pl.ds(start, blk), :]
        vb = v_ref[pl.ds(start, blk), :]
        off = slope * ((qi - kj) * blk).astype(F32)
        for c in range(2):
            s = _dot_nt(qm[c], kb) - srel - off
            if masked:
                s = jnp.where(rel >= 0, s, NEG)
            m_old = m_sc[c]
            m_new = jnp.maximum(m_old, jnp.max(s, axis=-1, keepdims=True))
            alpha = jnp.exp(m_old - m_new)
            p = jnp.exp(s - m_new)
            l_sc[c] = alpha * l_sc[c] + jnp.sum(p, axis=-1, keepdims=True)
            acc_sc[c] = alpha * acc_sc[c] + _dot(p.astype(BF16), vb)
            m_sc[c] = m_new

    def body(kj, carry):
        step(kj, False)
        return carry

    lax.fori_loop(0, qi, body, 0)
    step(qi, True)
    o = acc_sc[0] / l_sc[0] - lam * (acc_sc[1] / l_sc[1])
    on = o * lax.rsqrt(jnp.mean(o * o, axis=-1, keepdims=True) + EPS) * sub_ref[...]
    o_ref[...] = on * post_scale


N_SHIFT_PARTS = 3
FEAT_SUB = 64


def _attn_k_features(blk):
    j = np.arange(blk)
    f = np.zeros((blk, DV_A), np.float32)
    f[:, 0] = 1.0
    f[:, 1] = 1.0
    f[:, 2] = j // FEAT_SUB
    f[:, 3] = j % FEAT_SUB
    f[:, 4:4 + N_SHIFT_PARTS] = 1.0
    return jnp.asarray(f, dtype=BF16)


def _attn_q_features(blk, slopes_np, shift_parts):
    i = np.arange(blk)
    f = np.zeros((H_A, blk, DV_A), np.float32)
    for h in range(H_A):
        f[h, :, 0] = -slopes_np[h] * FEAT_SUB * (i // FEAT_SUB)
        f[h, :, 1] = -slopes_np[h] * (i % FEAT_SUB)
        f[h, :, 2] = slopes_np[h] * FEAT_SUB
        f[h, :, 3] = slopes_np[h]
    lane = np.arange(DV_A)
    out = jnp.asarray(f)
    for part in range(N_SHIFT_PARTS):
        out = jnp.where(lane == 4 + part, -shift_parts[part], out)
    return out.astype(BF16)


def _attn_shift_body(par_ref, slope_ref, q_ref, qf_ref, k_ref, v_ref, kf_ref, sub_ref, o_ref, acc_sc,
                     blk, post_scale, between_diagonal=None):
    hd = pl.program_id(1)
    qi = pl.program_id(2)
    slope = slope_ref[hd]
    lam = par_ref[0]
    q = q_ref[...]
    lane = lax.broadcasted_iota(jnp.int32, q.shape, 1)
    qfb = qf_ref[...]
    zero = jnp.zeros_like(q)
    qe = (jnp.concatenate([jnp.where(lane < DK_A, q, zero), qfb], axis=1),
          jnp.concatenate([jnp.where(lane >= DK_A, q, zero), qfb], axis=1))
    kf = kf_ref[...]
    ones = jnp.ones((blk, DV_A), BF16)

    def scores(kj):
        start = pl.multiple_of(kj * blk, blk)
        kb = jnp.concatenate([k_ref[pl.ds(start, blk), :], kf], axis=1)
        return [_dot_nt(qe[c], kb) for c in range(2)]

    def accumulate(kj, s, diagonal):
        start = pl.multiple_of(kj * blk, blk)
        vb = jnp.concatenate([v_ref[pl.ds(start, blk), :], ones], axis=1)
        cstep = slope * ((qi - kj) * blk).astype(F32)
        for c in range(2):
            p = jnp.exp(s[c] - cstep)
            if diagonal:
                rel = (lax.broadcasted_iota(jnp.int32, (blk, blk), 0)
                       - lax.broadcasted_iota(jnp.int32, (blk, blk), 1))
                pv = _dot(jnp.where(rel >= 0, p, 0.0).astype(BF16), vb)
                acc_sc[c] = pv
            else:
                acc_sc[c] += _dot(p.astype(BF16), vb)

    s_diag = scores(qi)
    if between_diagonal is not None:
        between_diagonal()
    accumulate(qi, s_diag, True)

    @pl.when(qi % 2 == 1)
    def _():
        accumulate(qi - 1, scores(qi - 1), False)

    def pair(i, carry):
        s_a = scores(2 * i)
        s_b = scores(2 * i + 1)
        accumulate(2 * i, s_a, False)
        accumulate(2 * i + 1, s_b, False)
        return carry

    lax.fori_loop(0, qi // 2, pair, 0)
    a0 = acc_sc[0]
    a1 = acc_sc[1]
    o = a0[:, 0:DV_A] / a0[:, DV_A:] - lam * (a1[:, 0:DV_A] / a1[:, DV_A:])
    on = o * lax.rsqrt(jnp.mean(o * o, axis=-1, keepdims=True) + EPS) * sub_ref[...]
    o_ref[...] = on * post_scale


def _attn_shift_kernel(par_ref, slope_ref, q_ref, qf_ref, k_ref, v_ref, kf_ref, sub_ref, o_ref, acc_sc,
                       *, blk, post_scale):
    _attn_shift_body(par_ref, slope_ref, q_ref, qf_ref, k_ref, v_ref, kf_ref, sub_ref, o_ref, acc_sc,
                     blk, post_scale)


def _attn_fused_kernel(pt_ref, par_ref, slope_ref, q_ref, qf_ref, k_ref, v_ref, kf_ref, sub_ref,
                       dslope_ref, dq_ref, dkn_ref, dvn_ref, *rest, blk, post_scale, n_pages):
    del pt_ref
    k_pages = rest[:n_pages]
    v_pages = rest[n_pages:2 * n_pages]
    o_ref, do_ref, acc_sc, s_sc = rest[2 * n_pages:]

    finish = _paged_attn_begin(dslope_ref[...], dq_ref[0], dkn_ref[0], dvn_ref[0], k_pages, s_sc,
                               n_pages)

    def decode_finish():
        do_ref[0] = finish(par_ref[0], sub_ref[...], v_pages, post_scale)

    _attn_shift_body(par_ref, slope_ref, q_ref, qf_ref, k_ref, v_ref, kf_ref, sub_ref, o_ref, acc_sc,
                     blk, post_scale, between_diagonal=decode_finish)


def _attn_prompt_specs(nq, seq, blk, index):
    smem = pl.BlockSpec(memory_space=pltpu.SMEM)
    return [
        smem, smem,
        pl.BlockSpec((blk, DV_A), index(lambda b, h, i: (b * nq + i, h))),
        pl.BlockSpec((None, blk, DV_A), index(lambda b, h, i: (h, 0, 0))),
        pl.BlockSpec((seq, DV_A), index(lambda b, h, i: (b, h))),
        pl.BlockSpec((seq, DV_A), index(lambda b, h, i: (b, h))),
        pl.BlockSpec((blk, DV_A), index(lambda b, h, i: (0, 0))),
        pl.BlockSpec((1, DV_A), index(lambda b, h, i: (0, 0))),
    ]


def _attn_prompt(par, slopes, q, qf, k, v, kf, sub, nb, seq, blk, post_scale, shifted):
    t = nb * seq
    nq = seq // blk
    if shifted:
        body = functools.partial(_attn_shift_kernel, blk=blk, post_scale=post_scale)
        scratch = [pltpu.VMEM((2, blk, 2 * DV_A), F32)]
        name = "attn_prompt_shift"
    else:
        body = functools.partial(_attn_kernel, blk=blk, post_scale=post_scale)
        scratch = [pltpu.VMEM((2, blk, 1), F32), pltpu.VMEM((2, blk, 1), F32),
                   pltpu.VMEM((2, blk, DV_A), F32)]
        name = "attn_prompt_online"
    return pl.pallas_call(
        body,
        grid=(nb, H_A, nq),
        in_specs=_attn_prompt_specs(nq, seq, blk, lambda f: f),
        out_specs=pl.BlockSpec((blk, DV_A), lambda b, h, i: (b * nq + i, h)),
        out_shape=jax.ShapeDtypeStruct((t, W_A), F32),
        scratch_shapes=scratch,
        name=name,
        compiler_params=pltpu.CompilerParams(
            dimension_semantics=("parallel", "parallel", "arbitrary"),
            vmem_limit_bytes=VMEM_LIMIT),
    )(par, slopes, q, qf, k, v, kf, sub)


def _attn_fused(pt_flat, par, slopes, q, qf, k, v, kf, sub, slope_rows, dq, dkn, dvn, ck, cv,
                layer, n_pages, nb, seq, blk, post_scale):
    t = nb * seq
    nq = seq // blk
    db = dq.shape[0]
    rows = PAGE_SIZE * H_A

    def with_pt(f):
        return lambda b, h, i, pt: f(b, h, i)

    def seq_index(b, h, i, pt):
        return ((b * H_A + h) * nq + i, 0, 0)

    def page_spec(j):
        def index(b, h, i, pt):
            return (layer, pt[((b * H_A + h) * nq + i) * n_pages + j], 0, 0)
        return pl.BlockSpec((None, None, rows, DV_A), index)

    in_specs = _attn_prompt_specs(nq, seq, blk, with_pt)
    in_specs += [pl.BlockSpec((ATT_ROWS, 1), lambda b, h, i, pt: (0, 0)),
                 pl.BlockSpec((1, 1, QK_A), seq_index),
                 pl.BlockSpec((1, 1, QK_A), seq_index),
                 pl.BlockSpec((1, 1, W_A), seq_index)]
    in_specs += [page_spec(j) for j in range(n_pages)] + [page_spec(j) for j in range(n_pages)]
    grid_spec = pltpu.PrefetchScalarGridSpec(
        num_scalar_prefetch=1,
        grid=(nb, H_A, nq),
        in_specs=in_specs,
        out_specs=[pl.BlockSpec((blk, DV_A), lambda b, h, i, pt: (b * nq + i, h)),
                   pl.BlockSpec((1, H_A, DV_A), seq_index)],
        scratch_shapes=[pltpu.VMEM((2, blk, 2 * DV_A), F32),
                        pltpu.VMEM((ATT_ROWS, n_pages * rows), F32)],
    )
    return pl.pallas_call(
        functools.partial(_attn_fused_kernel, blk=blk, post_scale=post_scale, n_pages=n_pages),
        grid_spec=grid_spec,
        out_shape=[jax.ShapeDtypeStruct((t, W_A), F32), jax.ShapeDtypeStruct((db, H_A, DV_A), F32)],
        name="attn_fused",
        compiler_params=pltpu.CompilerParams(
            dimension_semantics=("arbitrary", "arbitrary", "arbitrary"),
            vmem_limit_bytes=VMEM_LIMIT),
    )(pt_flat, par, slopes, q, qf, k, v, kf, sub, slope_rows,
      dq.reshape(db, 1, QK_A), dkn.reshape(db, 1, QK_A), dvn.reshape(db, 1, W_A),
      *([ck] * n_pages), *([cv] * n_pages))


def _block_diag_rows(m):
    head = lax.broadcasted_iota(jnp.int32, m.shape, 1) // DK_C
    parts = [jnp.where(head == hp, m, 0.0).astype(BF16) for hp in range(H_C)]
    return jnp.concatenate(parts, axis=0)


def _mix_kernel(x_ref, oa_ref, u_ref, g_ref, kk_ref, vc_ref, qh_ref, sg_ref,
                ltri_ref, hh_ref, pw_ref, ps_ref, hg_ref, wo_ref,
                xo_ref, pool_ref, hst_ref,
                ubuf, kkbuf, gbuf, vcbuf, st_sc, oc_sc, *, tc):
    si = pl.program_id(1)
    ns = pl.num_programs(1)
    pad = HGRN_SUB
    nchunk = tc // HGRN_CHUNK

    @pl.when(si == 0)
    def _():
        st_sc[...] = jnp.zeros(st_sc.shape, F32)
        z = jnp.zeros((pad, W_B), F32)
        ubuf[0:pad, :] = z
        kkbuf[0:pad, :] = z
        gbuf[0:pad, :] = z
        vcbuf[0:pad, :] = z
        oc_sc[...] = jnp.zeros(oc_sc.shape, F32)

    u = u_ref[...]
    ubuf[pad:pad + tc, :] = u
    acc = ubuf[...]
    wsum = {}
    for k, win in enumerate(POOL_WINDOWS):
        acc = acc + pltpu.roll(acc, 2 ** k, 0)
        wsum[win] = acc[pad:, :]
    pos = (si * tc + lax.broadcasted_iota(jnp.int32, (tc, 1), 0)).astype(F32)
    lane_b = lax.broadcasted_iota(jnp.int32, (tc, W_B), 1)
    pooled = None
    for gi, win in reversed(list(enumerate(POOL_WINDOWS))):
        term = wsum[win] * (1.0 / jnp.minimum(pos + 1.0, float(win)))
        pooled = term if pooled is None else jnp.where(lane_b < (gi + 1) * C_B, term, pooled)
    pooled = pooled - u
    ob = _dot(pooled.astype(BF16), pw_ref[...]) * ps_ref[...]
    ubuf[0:pad, :] = ubuf[tc:tc + pad, :]

    xo_ref[...] = (x_ref[...]
                   + _dot((oa_ref[...] * sg_ref[:, 0:W_A]).astype(BF16), wo_ref[0:W_A, :])
                   + _dot((ob * sg_ref[:, W_A:W_A + W_B]).astype(BF16), wo_ref[W_A:W_A + W_B, :]))

    g = g_ref[...]
    kk = kk_ref[...]
    vc = vc_ref[...]
    qh = qh_ref[...]
    ltri = ltri_ref[...]
    ghi, gmid, glo = _split3(g)
    gc = _dot(ltri, ghi) + _dot(ltri, gmid) + _dot(ltri, glo)
    gc3 = gc.reshape(nchunk, HGRN_CHUNK, W_C)

    def chunk_row(idx):
        r = jnp.broadcast_to(gc3[:, idx:idx + 1, :], (nchunk, HGRN_CHUNK, W_C))
        return r.reshape(tc, W_C)

    rc = lax.broadcasted_iota(jnp.int32, (tc, 1), 0) % HGRN_CHUNK
    hh = hh_ref[...]
    t_idx = lax.broadcasted_iota(jnp.int32, (HGRN_CHUNK, H_C * HGRN_CHUNK), 0)
    s_idx = lax.broadcasted_iota(jnp.int32, (HGRN_CHUNK, H_C * HGRN_CHUNK), 1) % HGRN_CHUNK
    chunks = [slice(c * HGRN_CHUNK, (c + 1) * HGRN_CHUNK) for c in range(nchunk)]

    rmid = chunk_row(HGRN_CHUNK // 2 - 1)
    safe = jnp.max(jnp.abs(gc - rmid)) <= HGRN_SAFE_EXP

    @pl.when(jnp.logical_not(safe))
    def _():
        upper = rc >= 32
        r31 = chunk_row(31)
        qt1 = jnp.where(upper, qh * jnp.exp(jnp.minimum(gc - r31, 0.0)), 0.0)
        kt1 = jnp.where(upper, 0.0, kk * jnp.exp(jnp.minimum(r31 - gc, 0.0)))
        ref2 = jnp.where(upper, chunk_row(47), chunk_row(15))
        odd = ((rc // HGRN_SUB) % 2) == 1
        qt2 = jnp.where(odd, qh * jnp.exp(jnp.minimum(gc - ref2, 0.0)), 0.0)
        kt2 = jnp.where(odd, 0.0, kk * jnp.exp(jnp.minimum(ref2 - gc, 0.0)))
        tb = t_idx // HGRN_SUB
        mask2 = ((tb % 2) == 1) & (s_idx // HGRN_SUB == tb - 1)

        kkbuf[pad:pad + tc, :] = kk
        gbuf[pad:pad + tc, :] = gc
        vcbuf[pad:pad + tc, :] = vc
        r16 = rc % HGRN_SUB
        od = jnp.zeros((tc, W_C), F32)
        for d in range(HGRN_SUB):
            kks = kkbuf[pad - d:pad - d + tc, :]
            gs = gbuf[pad - d:pad - d + tc, :]
            vcs = vcbuf[pad - d:pad - d + tc, :]
            xd = jnp.where(r16 >= d, qh * kks * jnp.exp(jnp.minimum(gc - gs, 0.0)), 0.0)
            od = od + _dot(xd.astype(BF16), hh) * vcs
        for sl in chunks:
            a1 = _dot_nt(qt1[sl].astype(BF16), _block_diag_rows(kt1[sl]))
            a2 = _dot_nt(qt2[sl].astype(BF16), _block_diag_rows(kt2[sl]))
            a_off = a1 + jnp.where(mask2, a2, 0.0)
            oc_sc[sl, :] = _dot(a_off.astype(BF16), _block_diag_rows(vc[sl])) + od[sl]

    eg = jnp.exp(gc)
    qe = qh * eg
    kdec = kk * jnp.exp(chunk_row(HGRN_CHUNK - 1) - gc)
    qt = qh * jnp.exp(gc - rmid)
    kt = kk * jnp.exp(rmid - gc)
    bdmask = (lax.broadcasted_iota(jnp.int32, (W_C, W_C), 0) // DK_C
              == lax.broadcasted_iota(jnp.int32, (W_C, W_C), 1) // DK_C)
    outs = []
    for c, sl in enumerate(chunks):
        a = _dot_nt(qt[sl].astype(BF16), _block_diag_rows(kt[sl]))
        a = jnp.where(t_idx >= s_idx, a, 0.0)
        intra = jnp.where(safe, _dot(a.astype(BF16), _block_diag_rows(vc[sl])), oc_sc[sl, :])
        st = st_sc[...]
        outs.append(intra + _dot_nt(qe[sl].astype(BF16), st.astype(BF16)))
        last = c * HGRN_CHUNK + HGRN_CHUNK - 1
        upd = _dot_tn(vc[sl].astype(BF16), kdec[sl].astype(BF16))
        st_sc[...] = st * eg[last:last + 1, :] + jnp.where(bdmask, upd, 0.0)

    o = jnp.concatenate(outs, axis=0)
    ss = _dot((o * o).astype(BF16), hh)
    ocn = o * lax.rsqrt(ss * (1.0 / DV_C) + EPS) * hg_ref[...]
    xo_ref[...] += _dot((ocn * sg_ref[:, W_A + W_B:]).astype(BF16), wo_ref[W_A + W_B:, :])

    @pl.when(si == ns - 1)
    def _():
        pool_ref[0] = u[tc - POOL_BUF:, :]
        hst_ref[0] = st_sc[...]


def _mix_prompt(x, oa, u, g, kk, vc, qh, sg, ltri, hh, pw, ps, hg, wo, nb, seq, tc):
    t = nb * seq
    ns = seq // tc
    row = lambda b, s: (b * ns + s, 0)
    fix = lambda b, s: (0, 0)
    in_specs = [pl.BlockSpec((tc, D_MODEL), row), pl.BlockSpec((tc, W_A), row)]
    in_specs += [pl.BlockSpec((tc, W_C), row) for _ in range(5)]
    in_specs += [
        pl.BlockSpec((tc, D_MODEL), row),
        pl.BlockSpec((tc, tc), fix),
        pl.BlockSpec((W_C, W_C), fix),
        pl.BlockSpec((W_B, W_B), fix),
        pl.BlockSpec((1, W_B), fix),
        pl.BlockSpec((1, W_C), fix),
        pl.BlockSpec((D_MODEL, D_MODEL), fix),
    ]
    return pl.pallas_call(
        functools.partial(_mix_kernel, tc=tc),
        grid=(nb, ns),
        in_specs=in_specs,
        out_specs=[
            pl.BlockSpec((tc, D_MODEL), row),
            pl.BlockSpec((1, POOL_BUF, W_B), lambda b, s: (b, 0, 0)),
            pl.BlockSpec((1, W_C, W_C), lambda b, s: (b, 0, 0)),
        ],
        out_shape=[
            jax.ShapeDtypeStruct((t, D_MODEL), F32),
            jax.ShapeDtypeStruct((nb, POOL_BUF, W_B), F32),
            jax.ShapeDtypeStruct((nb, W_C, W_C), F32),
        ],
        scratch_shapes=[
            pltpu.VMEM((tc + HGRN_SUB, W_B), F32),
            pltpu.VMEM((tc + HGRN_SUB, W_C), F32),
            pltpu.VMEM((tc + HGRN_SUB, W_C), F32),
            pltpu.VMEM((tc + HGRN_SUB, W_C), F32),
            pltpu.VMEM((W_C, W_C), F32),
            pltpu.VMEM((tc, W_C), F32),
        ],
        name="mix_prompt",
        compiler_params=pltpu.CompilerParams(
            dimension_semantics=("parallel", "arbitrary"), vmem_limit_bytes=VMEM_LIMIT),
    )(x, oa, u, g, kk, vc, qh, sg, ltri, hh, pw, ps, hg, wo)


ATT_ROWS = 2 * H_A


def _paged_attn_begin(slope, q_row, kn_row, vn_row, k_refs, s_sc, n_pages):
    rows = PAGE_SIZE * H_A
    past = n_pages * PAGE_SIZE

    r = lax.broadcasted_iota(jnp.int32, (ATT_ROWS, DV_A), 0)
    lane = lax.broadcasted_iota(jnp.int32, (ATT_ROWS, DV_A), 1)

    def head_rows(row):
        out = jnp.zeros((ATT_ROWS, DV_A), F32)
        for h in range(H_A):
            piece = jnp.broadcast_to(row[:, h * DV_A:(h + 1) * DV_A], (ATT_ROWS, DV_A))
            out = jnp.where(r % H_A == h, piece, out)
        return out

    q = jnp.where(lane // DK_A == r // (ATT_ROWS // 2), head_rows(q_row), 0.0)
    kn = head_rows(kn_row)
    vn = head_rows(vn_row)
    qb = q.astype(BF16)

    col = lax.broadcasted_iota(jnp.int32, (ATT_ROWS, rows), 1)
    rr = lax.broadcasted_iota(jnp.int32, (ATT_ROWS, rows), 0)
    valid = (col % H_A) == (rr % H_A)
    tok = col // H_A
    for j in range(n_pages):
        s = _dot_nt(qb, k_refs[j][...].astype(BF16))
        dist = (past - j * PAGE_SIZE - tok).astype(F32)
        s_sc[:, j * rows:(j + 1) * rows] = jnp.where(valid, s - slope * dist, NEG)
    s_self = jnp.sum(q * kn, axis=-1, keepdims=True)

    def finish(lam, sub, v_refs, post_scale):
        s_all = s_sc[...]
        m = jnp.maximum(jnp.max(s_all, axis=-1, keepdims=True), s_self)
        p_self = jnp.exp(s_self - m)
        l = p_self
        acc = p_self * vn
        for j in range(n_pages):
            p = jnp.exp(s_all[:, j * rows:(j + 1) * rows] - m)
            l = l + jnp.sum(p, axis=-1, keepdims=True)
            acc = acc + _dot(p.astype(BF16), v_refs[j][...].astype(BF16))
        o_maps = acc / l
        half = ATT_ROWS // 2
        o = o_maps[0:half] - lam * o_maps[half:ATT_ROWS]
        on = o * lax.rsqrt(jnp.mean(o * o, axis=-1, keepdims=True) + EPS) * sub
        return (on * post_scale)[0:H_A]

    return finish


def _paged_attn_kernel(pt_ref, lam_ref, slope_ref, q_ref, kn_ref, vn_ref, sub_ref, *rest,
                       n_pages, post_scale):
    del pt_ref
    k_refs = rest[:n_pages]
    v_refs = rest[n_pages:2 * n_pages]
    o_ref = rest[2 * n_pages]
    s_sc = rest[2 * n_pages + 1]
    finish = _paged_attn_begin(slope_ref[...], q_ref[0], kn_ref[0], vn_ref[0], k_refs, s_sc, n_pages)
    o_ref[0] = finish(lam_ref[0], sub_ref[...], v_refs, post_scale)


def _attn_paged(pt_flat, lam, slope_rows, q, kn, vn, sub, ck, cv, layer, n_pages, post_scale):
    nb = q.shape[0]
    rows = PAGE_SIZE * H_A
    smem = pl.BlockSpec(memory_space=pltpu.SMEM)
    row3 = lambda b, pt: (b, 0, 0)

    def page_spec(j):
        return pl.BlockSpec((None, None, rows, DV_A),
                            lambda b, pt, j=j: (layer, pt[b * n_pages + j], 0, 0))

    in_specs = [smem,
                pl.BlockSpec((ATT_ROWS, 1), lambda b, pt: (0, 0)),
                pl.BlockSpec((1, 1, QK_A), row3),
                pl.BlockSpec((1, 1, QK_A), row3),
                pl.BlockSpec((1, 1, W_A), row3),
                pl.BlockSpec((1, DV_A), lambda b, pt: (0, 0))]
    in_specs += [page_spec(j) for j in range(n_pages)] * 2
    grid_spec = pltpu.PrefetchScalarGridSpec(
        num_scalar_prefetch=1,
        grid=(nb,),
        in_specs=in_specs,
        out_specs=pl.BlockSpec((1, H_A, DV_A), row3),
        scratch_shapes=[pltpu.VMEM((ATT_ROWS, n_pages * rows), F32)],
    )
    return pl.pallas_call(
        functools.partial(_paged_attn_kernel, n_pages=n_pages, post_scale=post_scale),
        grid_spec=grid_spec,
        out_shape=jax.ShapeDtypeStruct((nb, H_A, DV_A), F32),
        name="attn_paged",
        compiler_params=pltpu.CompilerParams(
            dimension_semantics=("arbitrary",), vmem_limit_bytes=VMEM_LIMIT),
    )(pt_flat, lam, slope_rows, q.reshape(nb, 1, QK_A), kn.reshape(nb, 1, QK_A),
      vn.reshape(nb, 1, W_A), sub, *([ck] * n_pages), *([cv] * n_pages))


def _dec_mix_kernel(x_ref, oa_ref, u_ref, gt_ref, kkt_ref, qht_ref, vc_ref, sg_ref, sp_ref, sh_ref,
                    pw_ref, ps_ref, hg_ref, wo_ref,
                    xo_ref, po_ref, ho_ref, pooled_sc, o_sc, *, nbt, past):
    i = pl.program_id(0)
    ridx = lax.broadcasted_iota(jnp.int32, (POOL_BUF, W_B), 0)
    lane_b = lax.broadcasted_iota(jnp.int32, (1, W_B), 1)
    lane_t = lax.broadcasted_iota(jnp.int32, gt_ref.shape, 1)

    def per_batch(b, carry):
        prev = sp_ref[b]
        urow = u_ref[pl.ds(b, 1), :]
        pooled = None
        for gi, win in reversed(list(enumerate(POOL_WINDOWS))):
            ws = urow + jnp.sum(jnp.where(ridx >= POOL_BUF + 1 - win, prev, 0.0), axis=0, keepdims=True)
            term = ws * (1.0 / min(past + 1, win))
            pooled = term if pooled is None else jnp.where(lane_b < (gi + 1) * C_B, term, pooled)
        pooled_sc[pl.ds(b, 1), :] = pooled - urow
        po_ref[b, 0:POOL_BUF - 1, :] = prev[1:POOL_BUF, :]
        po_ref[b, POOL_BUF - 1:POOL_BUF, :] = urow

        bl = i * nbt + b

        def col(ref):
            return jnp.sum(jnp.where(lane_t == bl, ref[...], 0.0), axis=1, keepdims=True)

        fcol = jnp.exp(col(gt_ref))
        kcol = col(kkt_ref)
        qcol = col(qht_ref)
        vrow = vc_ref[pl.ds(b, 1), :]
        vrows = jnp.concatenate(
            [jnp.broadcast_to(vrow[:, h * DV_C:(h + 1) * DV_C], (DK_C, DV_C)) for h in range(H_C)],
            axis=0)
        s_new = fcol * sh_ref[b] + kcol * vrows
        ho_ref[b] = s_new
        o4 = jnp.sum((s_new * qcol).reshape(H_C, DK_C, DV_C), axis=1)
        for h in range(H_C):
            o_sc[h, pl.ds(b, 1), :] = o4[h:h + 1, :]
        return carry

    lax.fori_loop(0, nbt, per_batch, 0)

    ob = _dot(pooled_sc[...].astype(BF16), pw_ref[...]) * ps_ref[...]
    sg = sg_ref[...]
    y = (_dot((oa_ref[...] * sg[:, 0:W_A]).astype(BF16), wo_ref[0:W_A, :])
         + _dot((ob * sg[:, W_A:W_A + W_B]).astype(BF16), wo_ref[W_A:W_A + W_B, :]))
    for h in range(H_C):
        oh = o_sc[h]
        ohn = oh * lax.rsqrt(jnp.mean(oh * oh, axis=-1, keepdims=True) + EPS) * hg_ref[...]
        lo = W_A + W_B + h * DV_C
        y = y + _dot((ohn * sg[:, lo:lo + DV_C]).astype(BF16), wo_ref[lo:lo + DV_C, :])
    xo_ref[...] = x_ref[...] + y


def _mix_decode(x, oa, u, gt, kkt, qht, vc, sg, sp, sh, pw, ps, hg, wo, nbt, past):
    nb = x.shape[0]
    row = lambda i: (i, 0)
    fix = lambda i: (0, 0)
    row3 = lambda i: (i, 0, 0)
    in_specs = [
        pl.BlockSpec((nbt, D_MODEL), row),
        pl.BlockSpec((nbt, W_A), row),
        pl.BlockSpec((nbt, W_B), row),
        pl.BlockSpec((W_C, nb), fix),
        pl.BlockSpec((W_C, nb), fix),
        pl.BlockSpec((W_C, nb), fix),
        pl.BlockSpec((nbt, W_C), row),
        pl.BlockSpec((nbt, D_MODEL), row),
        pl.BlockSpec((nbt, POOL_BUF, W_B), row3),
        pl.BlockSpec((nbt, W_C, DV_C), row3),
        pl.BlockSpec((W_B, W_B), fix),
        pl.BlockSpec((1, W_B), fix),
        pl.BlockSpec((1, DV_C), fix),
        pl.BlockSpec((D_MODEL, D_MODEL), fix),
    ]
    return pl.pallas_call(
        functools.partial(_dec_mix_kernel, nbt=nbt, past=past),
        grid=(nb // nbt,),
        in_specs=in_specs,
        out_specs=[
            pl.BlockSpec((nbt, D_MODEL), row),
            pl.BlockSpec((nbt, POOL_BUF, W_B), row3),
            pl.BlockSpec((nbt, W_C, DV_C), row3),
        ],
        out_shape=[
            jax.ShapeDtypeStruct((nb, D_MODEL), F32),
            jax.ShapeDtypeStruct((nb, POOL_BUF, W_B), F32),
            jax.ShapeDtypeStruct((nb, W_C, DV_C), F32),
        ],
        scratch_shapes=[
            pltpu.VMEM((nbt, W_B), F32),
            pltpu.VMEM((H_C, nbt, DV_C), F32),
        ],
        name="mix_decode",
        compiler_params=pltpu.CompilerParams(
            dimension_semantics=("arbitrary",), vmem_limit_bytes=VMEM_LIMIT),
    )(x, oa, u, gt, kkt, qht, vc, sg, sp, sh, pw, ps, hg, wo)


def _block_ones(n, blk):
    idx = np.arange(n) // blk
    return jnp.asarray((idx[:, None] == idx[None, :]).astype(np.float32), dtype=BF16)


def _chunk_lower_tri(n, blk):
    idx = np.arange(n)
    same = (idx[:, None] // blk) == (idx[None, :] // blk)
    return jnp.asarray((same & (idx[:, None] >= idx[None, :])).astype(np.float32), dtype=BF16)


def kernel(x_prompt, x_sample, cache_k, cache_v, state_pool, state_hgrn, page_table,
           ln_gain, w_in, qn_gain, kn_gain, lam_q1, lam_k1, lam_q2, lam_k2, subln_gain,
           pool_w, pool_scale, hgrn_lb, hgrn_norm_gain, w_out):
    nb, seq, _ = x_prompt.shape
    t = nb * seq
    db = x_sample.shape[0]
    n_pages = page_table.shape[1]
    past = n_pages * PAGE_SIZE
    n_phys = cache_k.shape[1]

    tm = min(512, t)
    blk = min(512, seq)
    tc = min(256, seq)
    nbt = min(16, db)

    w_in_b = w_in.astype(BF16)
    w_out_b = w_out.astype(BF16)
    sm = jax.nn.softmax(hgrn_lb.astype(F32), axis=0)
    lb_all = jnp.cumsum(sm, axis=0) - sm[0]
    lam_init = [0.8 - 0.6 * math.exp(-0.3 * l) for l in range(DEPTH)]
    lam_all = (jnp.exp(jnp.sum(lam_q1.astype(F32) * lam_k1.astype(F32), axis=-1))
               - jnp.exp(jnp.sum(lam_q2.astype(F32) * lam_k2.astype(F32), axis=-1))
               + jnp.asarray(lam_init, F32))
    slopes_np = np.asarray([2.0 ** (-8.0 * (h + 1) / H_A) for h in range(H_A)], np.float32)
    slopes = jnp.asarray(slopes_np)
    slope_rows = jnp.asarray(slopes_np[np.arange(ATT_ROWS) % H_A].reshape(ATT_ROWS, 1))
    score_bound = (1.0125 * DK_A ** 0.5) * jnp.max(
        jnp.abs(qn_gain.astype(F32) * kn_gain.astype(F32)), axis=-1)
    sb_hi = score_bound.astype(BF16).astype(F32)
    sb_mid = (score_bound - sb_hi).astype(BF16).astype(F32)
    sb_lo = (score_bound - sb_hi - sb_mid).astype(BF16).astype(F32)
    attn_par = jnp.stack([lam_all, sb_hi, sb_mid, sb_lo], axis=1)
    kfeat = _attn_k_features(blk)
    fuse_decode = nb * H_A * (seq // blk) == db
    qg = jnp.tile(qn_gain.astype(F32), (1, QK_A // DK_A))
    kg = jnp.tile(kn_gain.astype(F32), (1, QK_A // DK_A))
    hg_tiled = jnp.tile(hgrn_norm_gain.astype(F32), (1, H_C))
    eye_g = jnp.eye(G_B, dtype=F32)
    pw_bd = jnp.einsum('lgcd,gh->lgchd', pool_w.astype(F32), eye_g).reshape(DEPTH, W_B, W_B).astype(BF16)
    bd_qk = _block_ones(QK_A, DK_A)
    hh = _block_ones(W_C, DV_C)
    ltri = _chunk_lower_tri(tc, HGRN_CHUNK)
    ck = cache_k.reshape(DEPTH, n_phys, PAGE_SIZE * H_A, 2 * DK_A)
    cv = cache_v.reshape(DEPTH, n_phys, PAGE_SIZE * H_A, DV_A)
    pt_flat = page_table.reshape(-1).astype(jnp.int32)

    xp = x_prompt.reshape(t, D_MODEL)
    xs = x_sample.reshape(db, D_MODEL)
    k_all = jnp.zeros((DEPTH, t * H_A, 2 * DK_A), F32)
    v_all = jnp.zeros((DEPTH, t * H_A, DV_A), F32)
    pools, states = [], []
    dks, dvs, dpools, dstates = [], [], [], []
    for l in range(DEPTH):
        lng = ln_gain[l].reshape(1, D_MODEL).astype(F32)
        lb = lb_all[l].reshape(1, W_C)
        sub = subln_gain[l].reshape(1, DV_A).astype(F32)
        ps = pool_scale[l].reshape(1, W_B).astype(F32)
        post = 1.0 - lam_init[l]

        q, k_all, kb, v_all, vb, u, g, kk, vc, qh, sg = _inproj_prompt(
            xp, lng, w_in_b[l], qg[l:l + 1], kg[l:l + 1], lb, bd_qk, k_all, v_all, l, tm)
        dq, dk, dv, du, dgt, dkkt, dvc, dqht, dsg = _inproj_decode(
            xs, lng, w_in_b[l], qg[l:l + 1], kg[l:l + 1], lb, bd_qk)

        def attend(par, q, kb, vb, sub, dq, dk, dv, shifted, layer=l, post=post):
            qf = _attn_q_features(blk, slopes_np, par[1:])
            if shifted and fuse_decode:
                return _attn_fused(pt_flat, par, slopes, q, qf, kb, vb, kfeat, sub, slope_rows,
                                   dq, dk, dv, ck, cv, layer, n_pages, nb, seq, blk, post)
            oa = _attn_prompt(par, slopes, q, qf, kb, vb, kfeat, sub, nb, seq, blk, post, shifted)
            doa = _attn_paged(pt_flat, par[0:1], slope_rows, dq, dk, dv, sub, ck, cv, layer,
                              n_pages, post)
            return oa, doa

        oa, doa = lax.cond(
            score_bound[l] <= ATTN_SAFE_SHIFT,
            functools.partial(attend, shifted=True), functools.partial(attend, shifted=False),
            attn_par[l], q, kb, vb, sub, dq, dk, dv)
        xp, pool_new, st = _mix_prompt(xp, oa, u, g, kk, vc, qh, sg, ltri, hh, pw_bd[l], ps,
                                       hg_tiled[l:l + 1], w_out_b[l], nb, seq, tc)
        pools.append(pool_new)
        states.append(st)
        xs, dpool, dstate = _mix_decode(
            xs, doa.reshape(db, W_A), du, dgt, dkkt, dqht, dvc, dsg, state_pool[l],
            state_hgrn[l].reshape(db, W_C, DV_C), pw_bd[l], ps,
            hgrn_norm_gain[l].reshape(1, DV_C).astype(F32), w_out_b[l], nbt, past)
        dks.append(dk)
        dvs.append(dv)
        dpools.append(dpool)
        dstates.append(dstate)

    y_prompt = xp.reshape(nb, seq, D_MODEL)
    y_sample = xs.reshape(db, 1, D_MODEL)
    k_prompt = k_all.reshape(DEPTH, nb, seq, H_A, 2 * DK_A)
    v_prompt = v_all.reshape(DEPTH, nb, seq, H_A, DV_A)
    k_sample = jnp.stack(dks).reshape(DEPTH, db, 1, H_A, 2 * DK_A)
    v_sample = jnp.stack(dvs).reshape(DEPTH, db, 1, H_A, DV_A)
    pool_prompt = jnp.stack(pools)
    pool_sample = jnp.stack(dpools)
    st_all = jnp.stack(states).reshape(DEPTH, nb, H_C, DV_C, H_C, DK_C)
    hgrn_prompt = jnp.stack([st_all[:, :, h, :, h, :] for h in range(H_C)], axis=2).swapaxes(-1, -2)
    hgrn_sample = jnp.stack(dstates).reshape(DEPTH, db, H_C, DK_C, DV_C)
    return (y_prompt, y_sample, k_prompt, v_prompt, k_sample, v_sample,
            pool_prompt, pool_sample, hgrn_prompt, hgrn_sample)
```

```python
import functools
import math

import numpy as np
import jax
import jax.numpy as jnp
from jax import lax
from jax.experimental import pallas as pl
from jax.experimental.pallas import tpu as pltpu

F32 = jnp.float32
BF16 = jnp.bfloat16

D_MODEL = 1024
DEPTH = 4
PAGE_SIZE = 128
H_A = 4
DK_A = 64
DV_A = 128
W_A = H_A * DV_A
QK_A = H_A * 2 * DK_A
G_B = 4
POOL_WINDOWS = (2, 4, 8, 16)
W_B = 256
C_B = W_B // G_B
POOL_BUF = max(POOL_WINDOWS) - 1
H_C = 4
W_C = 256
DK_C = 64
DV_C = 64
D_IN = 2 * QK_A + W_A + W_B + 3 * W_C + D_MODEL
HGRN_CHUNK = 64
HGRN_SUB = 16
HGRN_SAFE_EXP = 60.0
ATTN_SAFE_SHIFT = 20.0
EPS = 1e-6
NEG = -1e30
assert POOL_WINDOWS == tuple(2 ** (k + 1) for k in range(len(POOL_WINDOWS)))

C_Q, C_K, C_V, C_U = 0, QK_A, 2 * QK_A, 2 * QK_A + W_A
C_F = C_U + W_B
C_I = C_F + W_C
C_QC = C_I + W_C
C_GATE = C_QC + W_C

V7X_VMEM_BYTES = 64 * 1024 * 1024
VMEM_LIMIT = 56 * 1024 * 1024


def _dot(a, b):
    return jnp.dot(a, b, preferred_element_type=F32)


def _dot_nt(a, b):
    return lax.dot_general(a, b, (((1,), (1,)), ((), ())), preferred_element_type=F32)


def _dot_tn(a, b):
    return lax.dot_general(a, b, (((0,), (0,)), ((), ())), preferred_element_type=F32)


def _split3(x):
    hi = x.astype(BF16)
    r = x - hi.astype(F32)
    mid = r.astype(BF16)
    lo = (r - mid.astype(F32)).astype(BF16)
    return hi, mid, lo


def _sigmoid(x):
    return 1.0 / (1.0 + jnp.exp(-x))


def _inproj_body(x_ref, lng_ref, w_ref, qg_ref, kg_ref, lb_ref, bd_ref, channel_major_vc=False):
    x = x_ref[...]
    h = x * lax.rsqrt(jnp.mean(x * x, axis=-1, keepdims=True) + EPS) * lng_ref[...]
    hb = h.astype(BF16)
    bd = bd_ref[...]

    def proj(lo, hi):
        return _dot(hb, w_ref[:, lo:hi])

    def head_norm(z, gain):
        ss = _dot((z * z).astype(BF16), bd)
        return z * lax.rsqrt(ss * (1.0 / DK_A) + EPS) * gain

    qn = head_norm(proj(C_Q, C_K), qg_ref[...]) * (DK_A ** -0.5)
    kn = head_norm(proj(C_K, C_V), kg_ref[...])
    v = proj(C_V, C_U)
    u = proj(C_U, C_F)
    lb = lb_ref[...]
    sig = _sigmoid(proj(C_F, C_I))
    g = jnp.log(lb + (1.0 - lb) * sig)
    kk = (1.0 - lb) * (1.0 - sig)
    if channel_major_vc:
        vc = lax.dot_general(w_ref[:, C_I:C_QC], hb, (((0,), (1,)), ((), ())),
                             preferred_element_type=F32)
    else:
        vc = proj(C_I, C_QC)
    qc = proj(C_QC, C_GATE)
    qh = qc * _sigmoid(qc)
    gate = proj(C_GATE, D_IN)
    sg = gate * _sigmoid(gate)
    return qn, kn, v, u, g, kk, vc, qh, sg


def _inproj_prompt_kernel(x_ref, lng_ref, w_ref, qg_ref, kg_ref, lb_ref, bd_ref, kall_ref, vall_ref,
                          q_ref, k_ref, kb_ref, v_ref, vb_ref, u_ref, g_ref, kk_ref, vc_ref,
                          qh_ref, sg_ref):
    del kall_ref, vall_ref
    qn, kn, v, u, g, kk, vc, qh, sg = _inproj_body(x_ref, lng_ref, w_ref, qg_ref, kg_ref, lb_ref, bd_ref)
    tm = x_ref.shape[0]
    q_ref[...] = qn.astype(BF16)
    for h in range(H_A):
        k_ref[pl.ds(h, tm, stride=H_A), :] = kn[:, h * DV_A:(h + 1) * DV_A]
        v_ref[pl.ds(h, tm, stride=H_A), :] = v[:, h * DV_A:(h + 1) * DV_A]
    kb_ref[...] = kn.astype(BF16)
    vb_ref[...] = v.astype(BF16)
    u_ref[...] = u
    g_ref[...] = g
    kk_ref[...] = kk
    vc_ref[...] = vc
    qh_ref[...] = qh
    sg_ref[...] = sg


def _inproj_decode_kernel(x_ref, lng_ref, w_ref, qg_ref, kg_ref, lb_ref, bd_ref,
                          q_ref, k_ref, v_ref, u_ref, gt_ref, kkt_ref, vct_ref, qht_ref, sg_ref):
    qn, kn, v, u, g, kk, vct, qh, sg = _inproj_body(x_ref, lng_ref, w_ref, qg_ref, kg_ref, lb_ref, bd_ref,
                                                    channel_major_vc=True)
    q_ref[...] = qn
    k_ref[...] = kn
    v_ref[...] = v
    u_ref[...] = u
    gt_ref[...] = g.T
    kkt_ref[...] = kk.T
    vct_ref[...] = vct
    qht_ref[...] = qh.T
    sg_ref[...] = sg


def _inproj_in_specs(tm, layer):
    row = lambda i: (i, 0)
    fix = lambda i: (0, 0)
    return [
        pl.BlockSpec((tm, D_MODEL), row),
        pl.BlockSpec((1, D_MODEL), fix),
        pl.BlockSpec((None, D_MODEL, D_IN), lambda i: (layer, 0, 0)),
        pl.BlockSpec((1, QK_A), fix),
        pl.BlockSpec((1, QK_A), fix),
        pl.BlockSpec((1, W_C), fix),
        pl.BlockSpec((QK_A, QK_A), fix),
    ]


def _inproj_prompt(x, lng, w, qg, kg, lb, bd, k_all, v_all, layer, tm):
    t = x.shape[0]
    row = lambda i: (i, 0)

    def plain(wd, dt):
        return pl.BlockSpec((tm, wd), row), jax.ShapeDtypeStruct((t, wd), dt)

    def stacked(arr):
        return (pl.BlockSpec((None, tm * H_A, DV_A), lambda i: (layer, i, 0)),
                jax.ShapeDtypeStruct(arr.shape, arr.dtype))

    outs = [plain(QK_A, BF16), stacked(k_all), plain(QK_A, BF16), stacked(v_all), plain(W_A, BF16),
            plain(W_B, F32), plain(W_C, F32), plain(W_C, F32), plain(W_C, F32), plain(W_C, F32),
            plain(D_MODEL, F32)]
    n_in = len(_inproj_in_specs(tm, layer))
    any_spec = pl.BlockSpec(memory_space=pl.ANY)
    return pl.pallas_call(
        _inproj_prompt_kernel,
        grid=(t // tm,),
        in_specs=_inproj_in_specs(tm, layer) + [any_spec, any_spec],
        out_specs=[o[0] for o in outs],
        out_shape=[o[1] for o in outs],
        input_output_aliases={n_in: 1, n_in + 1: 3},
        name="inproj_prompt",
        compiler_params=pltpu.CompilerParams(
            dimension_semantics=("parallel",), vmem_limit_bytes=VMEM_LIMIT),
    )(x, lng, w, qg, kg, lb, bd, k_all, v_all)


def _inproj_decode(x, lng, w, qg, kg, lb, bd, layer):
    t = x.shape[0]
    row = lambda i: (i, 0)
    shapes = [(t, QK_A), (t, QK_A), (t, W_A), (t, W_B), (W_C, t), (W_C, t), (W_C, t), (W_C, t),
              (t, D_MODEL)]
    return pl.pallas_call(
        _inproj_decode_kernel,
        grid=(1,),
        in_specs=_inproj_in_specs(t, layer),
        out_specs=[pl.BlockSpec(s, row) for s in shapes],
        out_shape=[jax.ShapeDtypeStruct(s, F32) for s in shapes],
        name="inproj_decode",
        compiler_params=pltpu.CompilerParams(
            dimension_semantics=("arbitrary",), vmem_limit_bytes=VMEM_LIMIT),
    )(x, lng, w, qg, kg, lb, bd)


def _attn_kernel(par_ref, slope_ref, nkeep_ref, q_ref, qf_ref, k_ref, v_ref, kf_ref, sub_ref, o_ref,
                 m_sc, l_sc, acc_sc, *, blk, post_scale):
    del nkeep_ref, qf_ref, kf_ref
    hd = pl.program_id(1)
    qi = pl.program_id(2)
    slope = slope_ref[hd]
    lam = par_ref[0]
    q = q_ref[...]
    lane = lax.broadcasted_iota(jnp.int32, q.shape, 1)
    zero = jnp.zeros_like(q)
    qm = (jnp.where(lane < DK_A, q, zero), jnp.where(lane >= DK_A, q, zero))
    rel = (lax.broadcasted_iota(jnp.int32, (blk, blk), 0)
           - lax.broadcasted_iota(jnp.int32, (blk, blk), 1))
    srel = slope * rel.astype(F32)
    m_sc[...] = jnp.full(m_sc.shape, NEG, F32)
    l_sc[...] = jnp.zeros(l_sc.shape, F32)
    acc_sc[...] = jnp.zeros(acc_sc.shape, F32)

    def step(kj, masked):
        start = pl.multiple_of(kj * blk, blk)
        kb = k_ref[pl.ds(start, blk), :]
        vb = v_ref[pl.ds(start, blk), :]
        off = slope * ((qi - kj) * blk).astype(F32)
        for c in range(2):
            s = _dot_nt(qm[c], kb) - srel - off
            if masked:
                s = jnp.where(rel >= 0, s, NEG)
            m_old = m_sc[c]
            m_new = jnp.maximum(m_old, jnp.max(s, axis=-1, keepdims=True))
            alpha = jnp.exp(m_old - m_new)
            p = jnp.exp(s - m_new)
            l_sc[c] = alpha * l_sc[c] + jnp.sum(p, axis=-1, keepdims=True)
            acc_sc[c] = alpha * acc_sc[c] + _dot(p.astype(BF16), vb)
            m_sc[c] = m_new

    def body(kj, carry):
        step(kj, False)
        return carry

    lax.fori_loop(0, qi, body, 0)
    step(qi, True)
    o = acc_sc[0] / l_sc[0] - lam * (acc_sc[1] / l_sc[1])
    on = o * lax.rsqrt(jnp.mean(o * o, axis=-1, keepdims=True) + EPS) * sub_ref[...]
    o_ref[...] = on * post_scale


N_SHIFT_PARTS = 3
FEAT_SUB = 64


def _attn_k_features(blk):
    j = np.arange(blk)
    f = np.zeros((blk, DV_A), np.float32)
    f[:, 0] = 1.0
    f[:, 1] = 1.0
    f[:, 2] = j // FEAT_SUB
    f[:, 3] = j % FEAT_SUB
    f[:, 4:4 + N_SHIFT_PARTS] = 1.0
    return jnp.asarray(f, dtype=BF16)


ATTN_ZERO_EXP = -104.0


def _attn_blocks_kept(blk, nq, slopes_np):
    kept = []
    for slope in slopes_np:
        n = 1
        while n < nq and slope * ((n - 1) * blk + 1) <= -ATTN_ZERO_EXP:
            n += 1
        kept.append(n)
    return jnp.asarray(kept, jnp.int32)


def _attn_q_features(blk, slopes_np, shift_parts):
    i = np.arange(blk)
    f = np.zeros((H_A, blk, DV_A), np.float32)
    for h in range(H_A):
        f[h, :, 0] = -slopes_np[h] * FEAT_SUB * (i // FEAT_SUB)
        f[h, :, 1] = -slopes_np[h] * (i % FEAT_SUB)
        f[h, :, 2] = slopes_np[h] * FEAT_SUB
        f[h, :, 3] = slopes_np[h]
    lane = np.arange(DV_A)
    out = jnp.asarray(f)
    for part in range(N_SHIFT_PARTS):
        out = jnp.where(lane == 4 + part, -shift_parts[part], out)
    return out.astype(BF16)


def _attn_shift_body(par_ref, slope_ref, nkeep_ref, q_ref, qf_ref, k_ref, v_ref, kf_ref, sub_ref, o_ref, acc_sc,
                     blk, post_scale, between_diagonal=None):
    hd = pl.program_id(1)
    qi = pl.program_id(2)
    slope = slope_ref[hd]
    lam = par_ref[0]
    q = q_ref[...]
    lane = lax.broadcasted_iota(jnp.int32, q.shape, 1)
    qfb = qf_ref[...]
    zero = jnp.zeros_like(q)
    qe = (jnp.concatenate([jnp.where(lane < DK_A, q, zero), qfb], axis=1),
          jnp.concatenate([jnp.where(lane >= DK_A, q, zero), qfb], axis=1))
    kf = kf_ref[...]
    ones = jnp.ones((blk, DV_A), BF16)

    def scores(kj):
        start = pl.multiple_of(kj * blk, blk)
        kb = jnp.concatenate([k_ref[pl.ds(start, blk), :], kf], axis=1)
        return [_dot_nt(qe[c], kb) for c in range(2)]

    def accumulate(kj, s, diagonal):
        start = pl.multiple_of(kj * blk, blk)
        vb = jnp.concatenate([v_ref[pl.ds(start, blk), :], ones], axis=1)
        cstep = slope * ((qi - kj) * blk).astype(F32)
        for c in range(2):
            p = jnp.exp(s[c] - cstep)
            if diagonal:
                rel = (lax.broadcasted_iota(jnp.int32, (blk, blk), 0)
                       - lax.broadcasted_iota(jnp.int32, (blk, blk), 1))
                pv = _dot(jnp.where(rel >= 0, p, 0.0).astype(BF16), vb)
                acc_sc[c] = pv
            else:
                acc_sc[c] += _dot(p.astype(BF16), vb)

    s_diag = scores(qi)
    if between_diagonal is not None:
        between_diagonal()
    accumulate(qi, s_diag, True)

    first = jnp.maximum(qi + 1 - nkeep_ref[hd], 0)
    n_off = qi - first

    @pl.when(n_off % 2 == 1)
    def _():
        accumulate(qi - 1, scores(qi - 1), False)

    def pair(i, carry):
        ka = first + 2 * i
        s_a = scores(ka)
        s_b = scores(ka + 1)
        accumulate(ka, s_a, False)
        accumulate(ka + 1, s_b, False)
        return carry

    lax.fori_loop(0, n_off // 2, pair, 0)
    a0 = acc_sc[0]
    a1 = acc_sc[1]
    o = a0[:, 0:DV_A] / a0[:, DV_A:] - lam * (a1[:, 0:DV_A] / a1[:, DV_A:])
    on = o * lax.rsqrt(jnp.mean(o * o, axis=-1, keepdims=True) + EPS) * sub_ref[...]
    o_ref[...] = on * post_scale


def _attn_shift_kernel(par_ref, slope_ref, nkeep_ref, q_ref, qf_ref, k_ref, v_ref, kf_ref, sub_ref, o_ref, acc_sc,
                       *, blk, post_scale):
    _attn_shift_body(par_ref, slope_ref, nkeep_ref, q_ref, qf_ref, k_ref, v_ref, kf_ref, sub_ref, o_ref, acc_sc,
                     blk, post_scale)


def _attn_fused_kernel(pt_ref, par_ref, slope_ref, nkeep_ref, q_ref, qf_ref, k_ref, v_ref, kf_ref, sub_ref,
                       dslope_ref, dq_ref, dkn_ref, dvn_ref, *rest, blk, post_scale, n_pages):
    del pt_ref
    k_pages = rest[:n_pages]
    v_pages = rest[n_pages:2 * n_pages]
    o_ref, do_ref, acc_sc, s_sc = rest[2 * n_pages:]

    finish = _paged_attn_begin(dslope_ref[...], dq_ref[0], dkn_ref[0], dvn_ref[0], k_pages, s_sc,
                               n_pages)

    def decode_finish():
        do_ref[0] = finish(par_ref[0], sub_ref[...], v_pages, post_scale)

    _attn_shift_body(par_ref, slope_ref, nkeep_ref, q_ref, qf_ref, k_ref, v_ref, kf_ref, sub_ref, o_ref, acc_sc,
                     blk, post_scale, between_diagonal=decode_finish)


def _attn_prompt_specs(nq, seq, blk, index):
    smem = pl.BlockSpec(memory_space=pltpu.SMEM)
    return [
        smem, smem, smem,
        pl.BlockSpec((blk, DV_A), index(lambda b, h, i: (b * nq + i, h))),
        pl.BlockSpec((None, blk, DV_A), index(lambda b, h, i: (h, 0, 0))),
        pl.BlockSpec((seq, DV_A), index(lambda b, h, i: (b, h))),
        pl.BlockSpec((seq, DV_A), index(lambda b, h, i: (b, h))),
        pl.BlockSpec((blk, DV_A), index(lambda b, h, i: (0, 0))),
        pl.BlockSpec((1, DV_A), index(lambda b, h, i: (0, 0))),
    ]


def _attn_prompt(par, slopes, nkeep, q, qf, k, v, kf, sub, nb, seq, blk, post_scale, shifted):
    t = nb * seq
    nq = seq // blk
    if shifted:
        body = functools.partial(_attn_shift_kernel, blk=blk, post_scale=post_scale)
        scratch = [pltpu.VMEM((2, blk, 2 * DV_A), F32)]
        name = "attn_prompt_shift"
    else:
        body = functools.partial(_attn_kernel, blk=blk, post_scale=post_scale)
        scratch = [pltpu.VMEM((2, blk, 1), F32), pltpu.VMEM((2, blk, 1), F32),
                   pltpu.VMEM((2, blk, DV_A), F32)]
        name = "attn_prompt_online"
    return pl.pallas_call(
        body,
        grid=(nb, H_A, nq),
        in_specs=_attn_prompt_specs(nq, seq, blk, lambda f: f),
        out_specs=pl.BlockSpec((blk, DV_A), lambda b, h, i: (b * nq + i, h)),
        out_shape=jax.ShapeDtypeStruct((t, W_A), F32),
        scratch_shapes=scratch,
        name=name,
        compiler_params=pltpu.CompilerParams(
            dimension_semantics=("parallel", "parallel", "arbitrary"),
            vmem_limit_bytes=VMEM_LIMIT),
    )(par, slopes, nkeep, q, qf, k, v, kf, sub)


def _attn_fused(pt_flat, par, slopes, nkeep, q, qf, k, v, kf, sub, slope_rows, dq, dkn, dvn, ck, cv,
                layer, n_pages, nb, seq, blk, post_scale):
    t = nb * seq
    nq = seq // blk
    db = dq.shape[0]
    rows = PAGE_SIZE * H_A

    def with_pt(f):
        return lambda b, h, i, pt: f(b, h, i)

    def seq_index(b, h, i, pt):
        return ((b * H_A + h) * nq + i, 0, 0)

    def page_spec(j):
        def index(b, h, i, pt):
            return (layer, pt[((b * H_A + h) * nq + i) * n_pages + j], 0, 0)
        return pl.BlockSpec((None, None, rows, DV_A), index)

    in_specs = _attn_prompt_specs(nq, seq, blk, with_pt)
    in_specs += [pl.BlockSpec((ATT_ROWS, 1), lambda b, h, i, pt: (0, 0)),
                 pl.BlockSpec((1, 1, QK_A), seq_index),
                 pl.BlockSpec((1, 1, QK_A), seq_index),
                 pl.BlockSpec((1, 1, W_A), seq_index)]
    in_specs += [page_spec(j) for j in range(n_pages)] + [page_spec(j) for j in range(n_pages)]
    grid_spec = pltpu.PrefetchScalarGridSpec(
        num_scalar_prefetch=1,
        grid=(nb, H_A, nq),
        in_specs=in_specs,
        out_specs=[pl.BlockSpec((blk, DV_A), lambda b, h, i, pt: (b * nq + i, h)),
                   pl.BlockSpec((1, H_A, DV_A), seq_index)],
        scratch_shapes=[pltpu.VMEM((2, blk, 2 * DV_A), F32),
                        pltpu.VMEM((ATT_ROWS, n_pages * rows), F32)],
    )
    return pl.pallas_call(
        functools.partial(_attn_fused_kernel, blk=blk, post_scale=post_scale, n_pages=n_pages),
        grid_spec=grid_spec,
        out_shape=[jax.ShapeDtypeStruct((t, W_A), F32), jax.ShapeDtypeStruct((db, H_A, DV_A), F32)],
        name="attn_fused",
        compiler_params=pltpu.CompilerParams(
            dimension_semantics=("arbitrary", "arbitrary", "arbitrary"),
            vmem_limit_bytes=VMEM_LIMIT),
    )(pt_flat, par, slopes, nkeep, q, qf, k, v, kf, sub, slope_rows,
      dq.reshape(db, 1, QK_A), dkn.reshape(db, 1, QK_A), dvn.reshape(db, 1, W_A),
      *([ck] * n_pages), *([cv] * n_pages))


def _block_diag_rows(m):
    head = lax.broadcasted_iota(jnp.int32, m.shape, 1) // DK_C
    parts = [jnp.where(head == hp, m, 0.0).astype(BF16) for hp in range(H_C)]
    return jnp.concatenate(parts, axis=0)


def _mix_kernel(x_ref, oa_ref, u_ref, g_ref, kk_ref, vc_ref, qh_ref, sg_ref,
                ltri_ref, hh_ref, pw_ref, ps_ref, hg_ref, wo_ref,
                xo_ref, pool_ref, hst_ref,
                ubuf, kkbuf, gbuf, vcbuf, st_sc, oc_sc, *, tc):
    si = pl.program_id(1)
    ns = pl.num_programs(1)
    pad = HGRN_SUB
    nchunk = tc // HGRN_CHUNK

    @pl.when(si == 0)
    def _():
        st_sc[...] = jnp.zeros(st_sc.shape, F32)
        z = jnp.zeros((pad, W_B), F32)
        ubuf[0:pad, :] = z
        kkbuf[0:pad, :] = z
        gbuf[0:pad, :] = z
        vcbuf[0:pad, :] = z
        oc_sc[...] = jnp.zeros(oc_sc.shape, F32)

    u = u_ref[...]
    ubuf[pad:pad + tc, :] = u
    acc = ubuf[...]
    wsum = {}
    for k, win in enumerate(POOL_WINDOWS):
        acc = acc + pltpu.roll(acc, 2 ** k, 0)
        wsum[win] = acc[pad:, :]
    pos = (si * tc + lax.broadcasted_iota(jnp.int32, (tc, 1), 0)).astype(F32)
    lane_b = lax.broadcasted_iota(jnp.int32, (tc, W_B), 1)
    pooled = None
    for gi, win in reversed(list(enumerate(POOL_WINDOWS))):
        term = wsum[win] * (1.0 / jnp.minimum(pos + 1.0, float(win)))
        pooled = term if pooled is None else jnp.where(lane_b < (gi + 1) * C_B, term, pooled)
    pooled = pooled - u
    ob = _dot(pooled.astype(BF16), pw_ref[...]) * ps_ref[...]
    ubuf[0:pad, :] = ubuf[tc:tc + pad, :]

    xo_ref[...] = (x_ref[...]
                   + _dot((oa_ref[...] * sg_ref[:, 0:W_A]).astype(BF16), wo_ref[0:W_A, :])
                   + _dot((ob * sg_ref[:, W_A:W_A + W_B]).astype(BF16), wo_ref[W_A:W_A + W_B, :]))

    g = g_ref[...]
    kk = kk_ref[...]
    vc = vc_ref[...]
    qh = qh_ref[...]
    ltri = ltri_ref[...]
    ghi, gmid, glo = _split3(g)
    gc = _dot(ltri, ghi) + _dot(ltri, gmid) + _dot(ltri, glo)
    gc3 = gc.reshape(nchunk, HGRN_CHUNK, W_C)

    def chunk_row(idx):
        r = jnp.broadcast_to(gc3[:, idx:idx + 1, :], (nchunk, HGRN_CHUNK, W_C))
        return r.reshape(tc, W_C)

    rc = lax.broadcasted_iota(jnp.int32, (tc, 1), 0) % HGRN_CHUNK
    hh = hh_ref[...]
    t_idx = lax.broadcasted_iota(jnp.int32, (HGRN_CHUNK, H_C * HGRN_CHUNK), 0)
    s_idx = lax.broadcasted_iota(jnp.int32, (HGRN_CHUNK, H_C * HGRN_CHUNK), 1) % HGRN_CHUNK
    chunks = [slice(c * HGRN_CHUNK, (c + 1) * HGRN_CHUNK) for c in range(nchunk)]

    rmid = chunk_row(HGRN_CHUNK // 2 - 1)
    safe = jnp.max(jnp.abs(gc - rmid)) <= HGRN_SAFE_EXP

    @pl.when(jnp.logical_not(safe))
    def _():
        upper = rc >= 32
        r31 = chunk_row(31)
        qt1 = jnp.where(upper, qh * jnp.exp(jnp.minimum(gc - r31, 0.0)), 0.0)
        kt1 = jnp.where(upper, 0.0, kk * jnp.exp(jnp.minimum(r31 - gc, 0.0)))
        ref2 = jnp.where(upper, chunk_row(47), chunk_row(15))
        odd = ((rc // HGRN_SUB) % 2) == 1
        qt2 = jnp.where(odd, qh * jnp.exp(jnp.minimum(gc - ref2, 0.0)), 0.0)
        kt2 = jnp.where(odd, 0.0, kk * jnp.exp(jnp.minimum(ref2 - gc, 0.0)))
        tb = t_idx // HGRN_SUB
        mask2 = ((tb % 2) == 1) & (s_idx // HGRN_SUB == tb - 1)

        kkbuf[pad:pad + tc, :] = kk
        gbuf[pad:pad + tc, :] = gc
        vcbuf[pad:pad + tc, :] = vc
        r16 = rc % HGRN_SUB
        od = jnp.zeros((tc, W_C), F32)
        for d in range(HGRN_SUB):
            kks = kkbuf[pad - d:pad - d + tc, :]
            gs = gbuf[pad - d:pad - d + tc, :]
            vcs = vcbuf[pad - d:pad - d + tc, :]
            xd = jnp.where(r16 >= d, qh * kks * jnp.exp(jnp.minimum(gc - gs, 0.0)), 0.0)
            od = od + _dot(xd.astype(BF16), hh) * vcs
        for sl in chunks:
            a1 = _dot_nt(qt1[sl].astype(BF16), _block_diag_rows(kt1[sl]))
            a2 = _dot_nt(qt2[sl].astype(BF16), _block_diag_rows(kt2[sl]))
            a_off = a1 + jnp.where(mask2, a2, 0.0)
            oc_sc[sl, :] = _dot(a_off.astype(BF16), _block_diag_rows(vc[sl])) + od[sl]

    eg = jnp.exp(gc)
    qe = qh * eg
    kdec = kk * jnp.exp(chunk_row(HGRN_CHUNK - 1) - gc)
    qt = qh * jnp.exp(gc - rmid)
    kt = kk * jnp.exp(rmid - gc)
    bdmask = (lax.broadcasted_iota(jnp.int32, (W_C, W_C), 0) // DK_C
              == lax.broadcasted_iota(jnp.int32, (W_C, W_C), 1) // DK_C)
    outs = []
    for c, sl in enumerate(chunks):
        a = _dot_nt(qt[sl].astype(BF16), _block_diag_rows(kt[sl]))
        a = jnp.where(t_idx >= s_idx, a, 0.0)
        intra = jnp.where(safe, _dot(a.astype(BF16), _block_diag_rows(vc[sl])), oc_sc[sl, :])
        st = st_sc[...]
        outs.append(intra + _dot_nt(qe[sl].astype(BF16), st.astype(BF16)))
        last = c * HGRN_CHUNK + HGRN_CHUNK - 1
        upd = _dot_tn(vc[sl].astype(BF16), kdec[sl].astype(BF16))
        st_sc[...] = st * eg[last:last + 1, :] + jnp.where(bdmask, upd, 0.0)

    o = jnp.concatenate(outs, axis=0)
    ss = _dot((o * o).astype(BF16), hh)
    ocn = o * lax.rsqrt(ss * (1.0 / DV_C) + EPS) * hg_ref[...]
    xo_ref[...] += _dot((ocn * sg_ref[:, W_A + W_B:]).astype(BF16), wo_ref[W_A + W_B:, :])

    @pl.when(si == ns - 1)
    def _():
        pool_ref[0] = u[tc - POOL_BUF:, :]
        hst_ref[0] = st_sc[...]


def _mix_prompt(x, oa, u, g, kk, vc, qh, sg, ltri, hh, pw, ps, hg, wo, layer, nb, seq, tc):
    t = nb * seq
    ns = seq // tc
    row = lambda b, s: (b * ns + s, 0)
    fix = lambda b, s: (0, 0)
    lay = lambda b, s: (layer, 0, 0)
    in_specs = [pl.BlockSpec((tc, D_MODEL), row), pl.BlockSpec((tc, W_A), row)]
    in_specs += [pl.BlockSpec((tc, W_C), row) for _ in range(5)]
    in_specs += [
        pl.BlockSpec((tc, D_MODEL), row),
        pl.BlockSpec((tc, tc), fix),
        pl.BlockSpec((W_C, W_C), fix),
        pl.BlockSpec((None, W_B, W_B), lay),
        pl.BlockSpec((1, W_B), fix),
        pl.BlockSpec((1, W_C), fix),
        pl.BlockSpec((None, D_MODEL, D_MODEL), lay),
    ]
    return pl.pallas_call(
        functools.partial(_mix_kernel, tc=tc),
        grid=(nb, ns),
        in_specs=in_specs,
        out_specs=[
            pl.BlockSpec((tc, D_MODEL), row),
            pl.BlockSpec((1, POOL_BUF, W_B), lambda b, s: (b, 0, 0)),
            pl.BlockSpec((1, W_C, W_C), lambda b, s: (b, 0, 0)),
        ],
        out_shape=[
            jax.ShapeDtypeStruct((t, D_MODEL), F32),
            jax.ShapeDtypeStruct((nb, POOL_BUF, W_B), F32),
            jax.ShapeDtypeStruct((nb, W_C, W_C), F32),
        ],
        scratch_shapes=[
            pltpu.VMEM((tc + HGRN_SUB, W_B), F32),
            pltpu.VMEM((tc + HGRN_SUB, W_C), F32),
            pltpu.VMEM((tc + HGRN_SUB, W_C), F32),
            pltpu.VMEM((tc + HGRN_SUB, W_C), F32),
            pltpu.VMEM((W_C, W_C), F32),
            pltpu.VMEM((tc, W_C), F32),
        ],
        name="mix_prompt",
        compiler_params=pltpu.CompilerParams(
            dimension_semantics=("parallel", "arbitrary"), vmem_limit_bytes=VMEM_LIMIT),
    )(x, oa, u, g, kk, vc, qh, sg, ltri, hh, pw, ps, hg, wo)


ATT_ROWS = 2 * H_A


def _paged_attn_begin(slope, q_row, kn_row, vn_row, k_refs, s_sc, n_pages):
    rows = PAGE_SIZE * H_A
    past = n_pages * PAGE_SIZE

    r = lax.broadcasted_iota(jnp.int32, (ATT_ROWS, DV_A), 0)
    lane = lax.broadcasted_iota(jnp.int32, (ATT_ROWS, DV_A), 1)

    def head_rows(row):
        out = jnp.zeros((ATT_ROWS, DV_A), F32)
        for h in range(H_A):
            piece = jnp.broadcast_to(row[:, h * DV_A:(h + 1) * DV_A], (ATT_ROWS, DV_A))
            out = jnp.where(r % H_A == h, piece, out)
        return out

    q = jnp.where(lane // DK_A == r // (ATT_ROWS // 2), head_rows(q_row), 0.0)
    kn = head_rows(kn_row)
    vn = head_rows(vn_row)
    qb = q.astype(BF16)

    col = lax.broadcasted_iota(jnp.int32, (ATT_ROWS, rows), 1)
    rr = lax.broadcasted_iota(jnp.int32, (ATT_ROWS, rows), 0)
    valid = (col % H_A) == (rr % H_A)
    tok = col // H_A
    for j in range(n_pages):
        s = _dot_nt(qb, k_refs[j][...].astype(BF16))
        dist = (past - j * PAGE_SIZE - tok).astype(F32)
        s_sc[:, j * rows:(j + 1) * rows] = jnp.where(valid, s - slope * dist, NEG)
    s_self = jnp.sum(q * kn, axis=-1, keepdims=True)

    def finish(lam, sub, v_refs, post_scale):
        s_all = s_sc[...]
        m = jnp.maximum(jnp.max(s_all, axis=-1, keepdims=True), s_self)
        p_self = jnp.exp(s_self - m)
        l = p_self
        acc = p_self * vn
        for j in range(n_pages):
            p = jnp.exp(s_all[:, j * rows:(j + 1) * rows] - m)
            l = l + jnp.sum(p, axis=-1, keepdims=True)
            acc = acc + _dot(p.astype(BF16), v_refs[j][...].astype(BF16))
        o_maps = acc / l
        half = ATT_ROWS // 2
        o = o_maps[0:half] - lam * o_maps[half:ATT_ROWS]
        on = o * lax.rsqrt(jnp.mean(o * o, axis=-1, keepdims=True) + EPS) * sub
        return (on * post_scale)[0:H_A]

    return finish


def _paged_attn_kernel(pt_ref, lam_ref, slope_ref, q_ref, kn_ref, vn_ref, sub_ref, *rest,
                       n_pages, post_scale):
    del pt_ref
    k_refs = rest[:n_pages]
    v_refs = rest[n_pages:2 * n_pages]
    o_ref = rest[2 * n_pages]
    s_sc = rest[2 * n_pages + 1]
    finish = _paged_attn_begin(slope_ref[...], q_ref[0], kn_ref[0], vn_ref[0], k_refs, s_sc, n_pages)
    o_ref[0] = finish(lam_ref[0], sub_ref[...], v_refs, post_scale)


def _attn_paged(pt_flat, lam, slope_rows, q, kn, vn, sub, ck, cv, layer, n_pages, post_scale):
    nb = q.shape[0]
    rows = PAGE_SIZE * H_A
    smem = pl.BlockSpec(memory_space=pltpu.SMEM)
    row3 = lambda b, pt: (b, 0, 0)

    def page_spec(j):
        return pl.BlockSpec((None, None, rows, DV_A),
                            lambda b, pt, j=j: (layer, pt[b * n_pages + j], 0, 0))

    in_specs = [smem,
                pl.BlockSpec((ATT_ROWS, 1), lambda b, pt: (0, 0)),
                pl.BlockSpec((1, 1, QK_A), row3),
                pl.BlockSpec((1, 1, QK_A), row3),
                pl.BlockSpec((1, 1, W_A), row3),
                pl.BlockSpec((1, DV_A), lambda b, pt: (0, 0))]
    in_specs += [page_spec(j) for j in range(n_pages)] * 2
    grid_spec = pltpu.PrefetchScalarGridSpec(
        num_scalar_prefetch=1,
        grid=(nb,),
        in_specs=in_specs,
        out_specs=pl.BlockSpec((1, H_A, DV_A), row3),
        scratch_shapes=[pltpu.VMEM((ATT_ROWS, n_pages * rows), F32)],
    )
    return pl.pallas_call(
        functools.partial(_paged_attn_kernel, n_pages=n_pages, post_scale=post_scale),
        grid_spec=grid_spec,
        out_shape=jax.ShapeDtypeStruct((nb, H_A, DV_A), F32),
        name="attn_paged",
        compiler_params=pltpu.CompilerParams(
            dimension_semantics=("arbitrary",), vmem_limit_bytes=VMEM_LIMIT),
    )(pt_flat, lam, slope_rows, q.reshape(nb, 1, QK_A), kn.reshape(nb, 1, QK_A),
      vn.reshape(nb, 1, W_A), sub, *([ck] * n_pages), *([cv] * n_pages))


def _dec_mix_kernel(x_ref, oa_ref, u_ref, gt_ref, kkt_ref, vct_ref, qht_ref, sg_ref, sp_ref, sh_ref,
                    pall_ref, hall_ref, pw_ref, ps_ref, hgc_ref, wo_ref,
                    xo_ref, po_ref, ho_ref, ot_sc, *, past):
    del pall_ref, hall_ref
    h = pl.program_id(0)
    nb = x_ref.shape[0]

    @pl.when(h == 0)
    def _():
        u = u_ref[...]
        lane_b = lax.broadcasted_iota(jnp.int32, (nb, W_B), 1)
        acc = u
        wsum = {}
        for j in range(1, max(POOL_WINDOWS)):
            acc = acc + sp_ref[POOL_BUF - j]
            if j + 1 in POOL_WINDOWS:
                wsum[j + 1] = acc
        pooled = None
        for gi, win in reversed(list(enumerate(POOL_WINDOWS))):
            term = wsum[win] * (1.0 / min(past + 1, win))
            pooled = term if pooled is None else jnp.where(lane_b < (gi + 1) * C_B, term, pooled)
        pooled = pooled - u
        for j in range(POOL_BUF - 1):
            po_ref[j] = sp_ref[j + 1]
        po_ref[POOL_BUF - 1] = u
        ob = _dot(pooled.astype(BF16), pw_ref[...]) * ps_ref[...]
        xo_ref[...] = (x_ref[...]
                       + _dot((oa_ref[...] * sg_ref[:, 0:W_A]).astype(BF16), wo_ref[0:W_A, :])
                       + _dot((ob * sg_ref[:, W_A:W_A + W_B]).astype(BF16), wo_ref[W_A:W_A + W_B, :]))

    base = pl.multiple_of(h * DK_C, DK_C)
    vt = vct_ref[pl.ds(base, DV_C), :]

    def per_d(d, acc):
        r = base + d
        s_new = jnp.exp(gt_ref[pl.ds(r, 1), :]) * sh_ref[d] + kkt_ref[pl.ds(r, 1), :] * vt
        ho_ref[d] = s_new
        return acc + qht_ref[pl.ds(r, 1), :] * s_new

    ot_sc[pl.ds(base, DV_C), :] = lax.fori_loop(0, DK_C, per_d, jnp.zeros((DV_C, nb), F32))

    @pl.when(h == pl.num_programs(0) - 1)
    def _():
        ot = ot_sc[...].reshape(H_C, DV_C, nb)
        otn = ot * lax.rsqrt(jnp.mean(ot * ot, axis=1, keepdims=True) + EPS)
        mix_t = otn.reshape(W_C, nb) * hgc_ref[...] * sg_ref[:, W_A + W_B:].T
        xo_ref[...] += _dot_tn(mix_t.astype(BF16), wo_ref[W_A + W_B:, :])


def _mix_decode(x, oa, u, gt, kkt, vct, qht, sg, sp, sh, pool_all, hgrn_all, pw, ps, hgc, wo,
                layer, past):
    nb = x.shape[0]
    fix = lambda h: (0, 0)
    pool_spec = pl.BlockSpec((None, POOL_BUF, nb, W_B), lambda h: (layer, 0, 0, 0))
    state_spec = pl.BlockSpec((None, None, DK_C, DV_C, nb), lambda h: (layer, h, 0, 0, 0))
    any_spec = pl.BlockSpec(memory_space=pl.ANY)
    in_specs = [
        pl.BlockSpec((nb, D_MODEL), fix),
        pl.BlockSpec((nb, W_A), fix),
        pl.BlockSpec((nb, W_B), fix),
        pl.BlockSpec((W_C, nb), fix),
        pl.BlockSpec((W_C, nb), fix),
        pl.BlockSpec((W_C, nb), fix),
        pl.BlockSpec((W_C, nb), fix),
        pl.BlockSpec((nb, D_MODEL), fix),
        pool_spec,
        state_spec,
        any_spec,
        any_spec,
        pl.BlockSpec((None, W_B, W_B), lambda h: (layer, 0, 0)),
        pl.BlockSpec((1, W_B), fix),
        pl.BlockSpec((None, W_C, 1), lambda h: (layer, 0, 0)),
        pl.BlockSpec((None, D_MODEL, D_MODEL), lambda h: (layer, 0, 0)),
    ]
    return pl.pallas_call(
        functools.partial(_dec_mix_kernel, past=past),
        grid=(H_C,),
        in_specs=in_specs,
        out_specs=[pl.BlockSpec((nb, D_MODEL), fix), pool_spec, state_spec],
        out_shape=[
            jax.ShapeDtypeStruct((nb, D_MODEL), F32),
            jax.ShapeDtypeStruct(pool_all.shape, F32),
            jax.ShapeDtypeStruct(hgrn_all.shape, F32),
        ],
        scratch_shapes=[pltpu.VMEM((W_C, nb), F32)],
        input_output_aliases={10: 1, 11: 2},
        name="mix_decode",
        compiler_params=pltpu.CompilerParams(
            dimension_semantics=("arbitrary",), vmem_limit_bytes=VMEM_LIMIT),
    )(x, oa, u, gt, kkt, vct, qht, sg, sp, sh, pool_all, hgrn_all, pw, ps, hgc, wo)


def _block_ones(n, blk):
    idx = np.arange(n) // blk
    return jnp.asarray((idx[:, None] == idx[None, :]).astype(np.float32), dtype=BF16)


def _chunk_lower_tri(n, blk):
    idx = np.arange(n)
    same = (idx[:, None] // blk) == (idx[None, :] // blk)
    return jnp.asarray((same & (idx[:, None] >= idx[None, :])).astype(np.float32), dtype=BF16)


def kernel(x_prompt, x_sample, cache_k, cache_v, state_pool, state_hgrn, page_table,
           ln_gain, w_in, qn_gain, kn_gain, lam_q1, lam_k1, lam_q2, lam_k2, subln_gain,
           pool_w, pool_scale, hgrn_lb, hgrn_norm_gain, w_out):
    nb, seq, _ = x_prompt.shape
    t = nb * seq
    db = x_sample.shape[0]
    n_pages = page_table.shape[1]
    past = n_pages * PAGE_SIZE
    n_phys = cache_k.shape[1]

    tm = min(512, t)
    blk = min(512, seq)
    tc = min(256, seq)

    w_in_b = w_in.astype(BF16)
    w_out_b = w_out.astype(BF16)
    sm = jax.nn.softmax(hgrn_lb.astype(F32), axis=0)
    lb_all = jnp.cumsum(sm, axis=0) - sm[0]
    lam_init = [0.8 - 0.6 * math.exp(-0.3 * l) for l in range(DEPTH)]
    lam_all = (jnp.exp(jnp.sum(lam_q1.astype(F32) * lam_k1.astype(F32), axis=-1))
               - jnp.exp(jnp.sum(lam_q2.astype(F32) * lam_k2.astype(F32), axis=-1))
               + jnp.asarray(lam_init, F32))
    slopes_np = np.asarray([2.0 ** (-8.0 * (h + 1) / H_A) for h in range(H_A)], np.float32)
    slopes = jnp.asarray(slopes_np)
    slope_rows = jnp.asarray(slopes_np[np.arange(ATT_ROWS) % H_A].reshape(ATT_ROWS, 1))
    score_bound = (1.0125 * DK_A ** 0.5) * jnp.max(
        jnp.abs(qn_gain.astype(F32) * kn_gain.astype(F32)), axis=-1)
    sb_hi = score_bound.astype(BF16).astype(F32)
    sb_mid = (score_bound - sb_hi).astype(BF16).astype(F32)
    sb_lo = (score_bound - sb_hi - sb_mid).astype(BF16).astype(F32)
    attn_par = jnp.stack([lam_all, sb_hi, sb_mid, sb_lo], axis=1)
    kfeat = _attn_k_features(blk)
    nkeep = _attn_blocks_kept(blk, seq // blk, slopes_np)
    fuse_decode = nb * H_A * (seq // blk) == db
    qg = jnp.tile(qn_gain.astype(F32), (1, QK_A // DK_A))
    kg = jnp.tile(kn_gain.astype(F32), (1, QK_A // DK_A))
    hg_tiled = jnp.tile(hgrn_norm_gain.astype(F32), (1, H_C))
    eye_g = jnp.eye(G_B, dtype=F32)
    pw_bd = jnp.einsum('lgcd,gh->lgchd', pool_w.astype(F32), eye_g).reshape(DEPTH, W_B, W_B).astype(BF16)
    bd_qk = _block_ones(QK_A, DK_A)
    hh = _block_ones(W_C, DV_C)
    ltri = _chunk_lower_tri(tc, HGRN_CHUNK)
    ck = cache_k.reshape(DEPTH, n_phys, PAGE_SIZE * H_A, 2 * DK_A)
    cv = cache_v.reshape(DEPTH, n_phys, PAGE_SIZE * H_A, DV_A)
    pt_flat = page_table.reshape(-1).astype(jnp.int32)

    xp = x_prompt.reshape(t, D_MODEL)
    xs = x_sample.reshape(db, D_MODEL)
    k_all = jnp.zeros((DEPTH, t * H_A, 2 * DK_A), F32)
    v_all = jnp.zeros((DEPTH, t * H_A, DV_A), F32)
    pools, states = [], []
    dks, dvs = [], []
    sp_in = jnp.transpose(state_pool, (0, 2, 1, 3))
    sh_in = jnp.transpose(state_hgrn, (0, 2, 3, 4, 1))
    dpool_all = jnp.zeros(sp_in.shape, F32)
    dstate_all = jnp.zeros(sh_in.shape, F32)
    hg_col = jnp.tile(hgrn_norm_gain.astype(F32), (1, H_C)).reshape(DEPTH, W_C, 1)
    for l in range(DEPTH):
        lng = ln_gain[l].reshape(1, D_MODEL).astype(F32)
        lb = lb_all[l].reshape(1, W_C)
        sub = subln_gain[l].reshape(1, DV_A).astype(F32)
        ps = pool_scale[l].reshape(1, W_B).astype(F32)
        post = 1.0 - lam_init[l]

        q, k_all, kb, v_all, vb, u, g, kk, vc, qh, sg = _inproj_prompt(
            xp, lng, w_in_b, qg[l:l + 1], kg[l:l + 1], lb, bd_qk, k_all, v_all, l, tm)
        dq, dk, dv, du, dgt, dkkt, dvct, dqht, dsg = _inproj_decode(
            xs, lng, w_in_b, qg[l:l + 1], kg[l:l + 1], lb, bd_qk, l)

        def attend(par, q, kb, vb, sub, dq, dk, dv, shifted, layer=l, post=post):
            qf = _attn_q_features(blk, slopes_np, par[1:])
            if shifted and fuse_decode:
                return _attn_fused(pt_flat, par, slopes, nkeep, q, qf, kb, vb, kfeat, sub, slope_rows,
                                   dq, dk, dv, ck, cv, layer, n_pages, nb, seq, blk, post)
            oa = _attn_prompt(par, slopes, nkeep, q, qf, kb, vb, kfeat, sub, nb, seq, blk, post, shifted)
            doa = _attn_paged(pt_flat, par[0:1], slope_rows, dq, dk, dv, sub, ck, cv, layer,
                              n_pages, post)
            return oa, doa

        oa, doa = lax.cond(
            score_bound[l] <= ATTN_SAFE_SHIFT,
            functools.partial(attend, shifted=True), functools.partial(attend, shifted=False),
            attn_par[l], q, kb, vb, sub, dq, dk, dv)
        xp, pool_new, st = _mix_prompt(xp, oa, u, g, kk, vc, qh, sg, ltri, hh, pw_bd, ps,
                                       hg_tiled[l:l + 1], w_out_b, l, nb, seq, tc)
        pools.append(pool_new)
        states.append(st)
        xs, dpool_all, dstate_all = _mix_decode(
            xs, doa.reshape(db, W_A), du, dgt, dkkt, dvct, dqht, dsg, sp_in, sh_in,
            dpool_all, dstate_all, pw_bd, ps, hg_col, w_out_b, l, past)
        dks.append(dk)
        dvs.append(dv)

    y_prompt = xp.reshape(nb, seq, D_MODEL)
    y_sample = xs.reshape(db, 1, D_MODEL)
    k_prompt = k_all.reshape(DEPTH, nb, seq, H_A, 2 * DK_A)
    v_prompt = v_all.reshape(DEPTH, nb, seq, H_A, DV_A)
    k_sample = jnp.stack(dks).reshape(DEPTH, db, 1, H_A, 2 * DK_A)
    v_sample = jnp.stack(dvs).reshape(DEPTH, db, 1, H_A, DV_A)
    pool_prompt = jnp.stack(pools)
    pool_sample = jnp.transpose(dpool_all, (0, 2, 1, 3))
    st_all = jnp.stack(states).reshape(DEPTH, nb, H_C, DV_C, H_C, DK_C)
    hgrn_prompt = jnp.stack([st_all[:, :, h, :, h, :] for h in range(H_C)], axis=2).swapaxes(-1, -2)
    hgrn_sample = jnp.transpose(dstate_all, (0, 4, 1, 2, 3))
    return (y_prompt, y_sample, k_prompt, v_prompt, k_sample, v_sample,
            pool_prompt, pool_sample, hgrn_prompt, hgrn_sample)
```

```python
import functools
import math

import numpy as np
import jax
import jax.numpy as jnp
from jax import lax
from jax.experimental import pallas as pl
from jax.experimental.pallas import tpu as pltpu

F32 = jnp.float32
BF16 = jnp.bfloat16

D_MODEL = 1024
DEPTH = 4
PAGE_SIZE = 128
H_A = 4
DK_A = 64
DV_A = 128
W_A = H_A * DV_A
QK_A = H_A * 2 * DK_A
G_B = 4
POOL_WINDOWS = (2, 4, 8, 16)
W_B = 256
C_B = W_B // G_B
POOL_BUF = max(POOL_WINDOWS) - 1
H_C = 4
W_C = 256
DK_C = 64
DV_C = 64
D_IN = 2 * QK_A + W_A + W_B + 3 * W_C + D_MODEL
HGRN_CHUNK = 64
HGRN_SUB = 16
HGRN_SAFE_EXP = 60.0
ATTN_SAFE_SHIFT = 20.0
EPS = 1e-6
NEG = -1e30
assert POOL_WINDOWS == tuple(2 ** (k + 1) for k in range(len(POOL_WINDOWS)))

C_Q, C_K, C_V, C_U = 0, QK_A, 2 * QK_A, 2 * QK_A + W_A
C_F = C_U + W_B
C_I = C_F + W_C
C_QC = C_I + W_C
C_GATE = C_QC + W_C

V7X_VMEM_BYTES = 64 * 1024 * 1024
VMEM_LIMIT = 56 * 1024 * 1024


def _dot(a, b):
    return jnp.dot(a, b, preferred_element_type=F32)


def _dot_nt(a, b):
    return lax.dot_general(a, b, (((1,), (1,)), ((), ())), preferred_element_type=F32)


def _dot_tn(a, b):
    return lax.dot_general(a, b, (((0,), (0,)), ((), ())), preferred_element_type=F32)


def _split3(x):
    hi = x.astype(BF16)
    r = x - hi.astype(F32)
    mid = r.astype(BF16)
    lo = (r - mid.astype(F32)).astype(BF16)
    return hi, mid, lo


def _sigmoid(x):
    return 1.0 / (1.0 + jnp.exp(-x))


def _inproj_body(x_ref, lng_ref, w_ref, qg_ref, kg_ref, lb_ref, bd_ref, channel_major_vc=False):
    x = x_ref[...]
    h = x * lax.rsqrt(jnp.mean(x * x, axis=-1, keepdims=True) + EPS) * lng_ref[...]
    hb = h.astype(BF16)
    bd = bd_ref[...]

    def proj(lo, hi):
        return _dot(hb, w_ref[:, lo:hi])

    def head_norm(z, gain):
        sq = (z * z).astype(BF16)
        bw = bd.shape[0]
        ss = jnp.concatenate([_dot(sq[:, j:j + bw], bd) for j in range(0, z.shape[1], bw)], axis=1)
        return z * lax.rsqrt(ss * (1.0 / DK_A) + EPS) * gain

    qn = head_norm(proj(C_Q, C_K), qg_ref[...]) * (DK_A ** -0.5)
    kn = head_norm(proj(C_K, C_V), kg_ref[...])
    v = proj(C_V, C_U)
    u = proj(C_U, C_F)
    lb = lb_ref[...]
    sig = _sigmoid(proj(C_F, C_I))
    g = jnp.log(lb + (1.0 - lb) * sig)
    kk = (1.0 - lb) * (1.0 - sig)
    if channel_major_vc:
        vc = lax.dot_general(w_ref[:, C_I:C_QC], hb, (((0,), (1,)), ((), ())),
                             preferred_element_type=F32)
    else:
        vc = proj(C_I, C_QC)
    qc = proj(C_QC, C_GATE)
    qh = qc * _sigmoid(qc)
    gate = proj(C_GATE, D_IN)
    sg = gate * _sigmoid(gate)
    return qn, kn, v, u, g, kk, vc, qh, sg


def _inproj_prompt_kernel(x_ref, lng_ref, w_ref, qg_ref, kg_ref, lb_ref, bd_ref, kall_ref, vall_ref,
                          q_ref, k_ref, kb_ref, v_ref, vb_ref, u_ref, g_ref, kk_ref, vc_ref,
                          qh_ref, sg_ref):
    del kall_ref, vall_ref
    qn, kn, v, u, g, kk, vc, qh, sg = _inproj_body(x_ref, lng_ref, w_ref, qg_ref, kg_ref, lb_ref, bd_ref)
    tm = x_ref.shape[0]
    q_ref[...] = qn.astype(BF16)
    for h in range(H_A):
        k_ref[pl.ds(h, tm, stride=H_A), :] = kn[:, h * DV_A:(h + 1) * DV_A]
        v_ref[pl.ds(h, tm, stride=H_A), :] = v[:, h * DV_A:(h + 1) * DV_A]
    kb_ref[...] = kn.astype(BF16)
    vb_ref[...] = v.astype(BF16)
    u_ref[...] = u
    g_ref[...] = g
    kk_ref[...] = kk
    vc_ref[...] = vc
    qh_ref[...] = qh
    sg_ref[...] = sg


def _inproj_decode_kernel(x_ref, lng_ref, w_ref, qg_ref, kg_ref, lb_ref, bd_ref,
                          q_ref, k_ref, v_ref, u_ref, gt_ref, kkt_ref, vct_ref, qht_ref, sg_ref):
    qn, kn, v, u, g, kk, vct, qh, sg = _inproj_body(x_ref, lng_ref, w_ref, qg_ref, kg_ref, lb_ref, bd_ref,
                                                    channel_major_vc=True)
    q_ref[...] = qn
    k_ref[...] = kn
    v_ref[...] = v
    u_ref[...] = u
    gt_ref[...] = g.T
    kkt_ref[...] = kk.T
    vct_ref[...] = vct
    qht_ref[...] = qh.T
    sg_ref[...] = sg


def _inproj_in_specs(tm, layer):
    row = lambda i: (i, 0)
    fix = lambda i: (0, 0)
    return [
        pl.BlockSpec((tm, D_MODEL), row),
        pl.BlockSpec((1, D_MODEL), fix),
        pl.BlockSpec((None, D_MODEL, D_IN), lambda i: (layer, 0, 0)),
        pl.BlockSpec((1, QK_A), fix),
        pl.BlockSpec((1, QK_A), fix),
        pl.BlockSpec((1, W_C), fix),
        pl.BlockSpec((W_C, W_C), fix),
    ]


def _inproj_prompt(x, lng, w, qg, kg, lb, bd, k_all, v_all, layer, tm):
    t = x.shape[0]
    row = lambda i: (i, 0)

    def plain(wd, dt):
        return pl.BlockSpec((tm, wd), row), jax.ShapeDtypeStruct((t, wd), dt)

    def stacked(arr):
        return (pl.BlockSpec((None, tm * H_A, DV_A), lambda i: (layer, i, 0)),
                jax.ShapeDtypeStruct(arr.shape, arr.dtype))

    outs = [plain(QK_A, BF16), stacked(k_all), plain(QK_A, BF16), stacked(v_all), plain(W_A, BF16),
            plain(W_B, F32), plain(W_C, F32), plain(W_C, F32), plain(W_C, F32), plain(W_C, F32),
            plain(D_MODEL, F32)]
    n_in = len(_inproj_in_specs(tm, layer))
    any_spec = pl.BlockSpec(memory_space=pl.ANY)
    return pl.pallas_call(
        _inproj_prompt_kernel,
        grid=(t // tm,),
        in_specs=_inproj_in_specs(tm, layer) + [any_spec, any_spec],
        out_specs=[o[0] for o in outs],
        out_shape=[o[1] for o in outs],
        input_output_aliases={n_in: 1, n_in + 1: 3},
        name="inproj_prompt",
        compiler_params=pltpu.CompilerParams(
            dimension_semantics=("parallel",), vmem_limit_bytes=VMEM_LIMIT),
    )(x, lng, w, qg, kg, lb, bd, k_all, v_all)


def _inproj_decode(x, lng, w, qg, kg, lb, bd, layer):
    t = x.shape[0]
    row = lambda i: (i, 0)
    shapes = [(t, QK_A), (t, QK_A), (t, W_A), (t, W_B), (W_C, t), (W_C, t), (W_C, t), (W_C, t),
              (t, D_MODEL)]
    return pl.pallas_call(
        _inproj_decode_kernel,
        grid=(1,),
        in_specs=_inproj_in_specs(t, layer),
        out_specs=[pl.BlockSpec(s, row) for s in shapes],
        out_shape=[jax.ShapeDtypeStruct(s, F32) for s in shapes],
        name="inproj_decode",
        compiler_params=pltpu.CompilerParams(
            dimension_semantics=("arbitrary",), vmem_limit_bytes=VMEM_LIMIT),
    )(x, lng, w, qg, kg, lb, bd)


def _attn_kernel(par_ref, slope_ref, nkeep_ref, q_ref, qf_ref, k_ref, v_ref, kf_ref, sub_ref, o_ref,
                 m_sc, l_sc, acc_sc, *, blk, post_scale):
    del nkeep_ref, qf_ref, kf_ref
    hd = pl.program_id(1)
    qi = pl.program_id(2)
    slope = slope_ref[hd]
    lam = par_ref[0]
    q = q_ref[...]
    lane = lax.broadcasted_iota(jnp.int32, q.shape, 1)
    zero = jnp.zeros_like(q)
    qm = (jnp.where(lane < DK_A, q, zero), jnp.where(lane >= DK_A, q, zero))
    rel = (lax.broadcasted_iota(jnp.int32, (blk, blk), 0)
           - lax.broadcasted_iota(jnp.int32, (blk, blk), 1))
    srel = slope * rel.astype(F32)
    m_sc[...] = jnp.full(m_sc.shape, NEG, F32)
    l_sc[...] = jnp.zeros(l_sc.shape, F32)
    acc_sc[...] = jnp.zeros(acc_sc.shape, F32)

    def step(kj, masked):
        start = pl.multiple_of(kj * blk, blk)
        kb = k_ref[pl.ds(start, blk), :]
        vb = v_ref[pl.ds(start, blk), :]
        off = slope * ((qi - kj) * blk).astype(F32)
        for c in range(2):
            s = _dot_nt(qm[c], kb) - srel - off
            if masked:
                s = jnp.where(rel >= 0, s, NEG)
            m_old = m_sc[c]
            m_new = jnp.maximum(m_old, jnp.max(s, axis=-1, keepdims=True))
            alpha = jnp.exp(m_old - m_new)
            p = jnp.exp(s - m_new)
            l_sc[c] = alpha * l_sc[c] + jnp.sum(p, axis=-1, keepdims=True)
            acc_sc[c] = alpha * acc_sc[c] + _dot(p.astype(BF16), vb)
            m_sc[c] = m_new

    def body(kj, carry):
        step(kj, False)
        return carry

    lax.fori_loop(0, qi, body, 0)
    step(qi, True)
    o = acc_sc[0] / l_sc[0] - lam * (acc_sc[1] / l_sc[1])
    on = o * lax.rsqrt(jnp.mean(o * o, axis=-1, keepdims=True) + EPS) * sub_ref[...]
    o_ref[...] = on * post_scale


N_SHIFT_PARTS = 3
FEAT_SUB = 64


def _attn_k_features(blk):
    j = np.arange(blk)
    f = np.zeros((blk, DV_A), np.float32)
    f[:, 0] = 1.0
    f[:, 1] = 1.0
    f[:, 2] = j // FEAT_SUB
    f[:, 3] = j % FEAT_SUB
    f[:, 4:4 + N_SHIFT_PARTS] = 1.0
    return jnp.asarray(f, dtype=BF16)


ATTN_ZERO_EXP = -104.0


def _attn_blocks_kept(blk, nq, slopes_np):
    kept = []
    for slope in slopes_np:
        n = 1
        while n < nq and slope * ((n - 1) * blk + 1) <= -ATTN_ZERO_EXP:
            n += 1
        kept.append(n)
    return jnp.asarray(kept, jnp.int32)


def _attn_q_features(blk, slopes_np, shift_parts):
    i = np.arange(blk)
    f = np.zeros((H_A, blk, DV_A), np.float32)
    for h in range(H_A):
        f[h, :, 0] = -slopes_np[h] * FEAT_SUB * (i // FEAT_SUB)
        f[h, :, 1] = -slopes_np[h] * (i % FEAT_SUB)
        f[h, :, 2] = slopes_np[h] * FEAT_SUB
        f[h, :, 3] = slopes_np[h]
    lane = np.arange(DV_A)
    out = jnp.asarray(f)
    for part in range(N_SHIFT_PARTS):
        out = jnp.where(lane == 4 + part, -shift_parts[part], out)
    return out.astype(BF16)


def _attn_shift_body(hd, qi, par_ref, slope_ref, nkeep_ref, q_ref, qf_ref, k_ref, v_ref, kf_ref, sub_ref,
                     o_ref, acc_sc, blk, post_scale, between_diagonal=None):
    slope = slope_ref[hd]
    lam = par_ref[0]
    q = q_ref[...]
    lane = lax.broadcasted_iota(jnp.int32, q.shape, 1)
    qfb = qf_ref[...]
    zero = jnp.zeros_like(q)
    qe = (jnp.concatenate([jnp.where(lane < DK_A, q, zero), qfb], axis=1),
          jnp.concatenate([jnp.where(lane >= DK_A, q, zero), qfb], axis=1))
    kf = kf_ref[...]
    ones = jnp.ones((blk, DV_A), BF16)

    def scores(kj):
        start = pl.multiple_of(kj * blk, blk)
        kb = jnp.concatenate([k_ref[pl.ds(start, blk), :], kf], axis=1)
        return [_dot_nt(qe[c], kb) for c in range(2)]

    def accumulate(kj, s):
        start = pl.multiple_of(kj * blk, blk)
        vb = jnp.concatenate([v_ref[pl.ds(start, blk), :], ones], axis=1)
        cstep = slope * ((qi - kj) * blk).astype(F32)
        for c in range(2):
            acc_sc[c] += _dot(jnp.exp(s[c] - cstep).astype(BF16), vb)

    half = blk // 2
    diag_start = pl.multiple_of(qi * blk, blk)
    kb_diag = jnp.concatenate([k_ref[pl.ds(diag_start, blk), :], kf], axis=1)
    s_top = [_dot_nt(qe[c][0:half], kb_diag[0:half]) for c in range(2)]
    s_bot = [_dot_nt(qe[c][half:blk], kb_diag) for c in range(2)]
    if between_diagonal is not None:
        between_diagonal()
    vb_diag = jnp.concatenate([v_ref[pl.ds(diag_start, blk), :], ones], axis=1)
    rel_top = (lax.broadcasted_iota(jnp.int32, (half, half), 0)
               - lax.broadcasted_iota(jnp.int32, (half, half), 1))
    rel_bot = (lax.broadcasted_iota(jnp.int32, (half, blk), 0) + half
               - lax.broadcasted_iota(jnp.int32, (half, blk), 1))
    for c in range(2):
        p_top = jnp.where(rel_top >= 0, jnp.exp(s_top[c]), 0.0).astype(BF16)
        p_bot = jnp.where(rel_bot >= 0, jnp.exp(s_bot[c]), 0.0).astype(BF16)
        acc_sc[c, 0:half, :] = _dot(p_top, vb_diag[0:half])
        acc_sc[c, half:blk, :] = _dot(p_bot, vb_diag)

    first = jnp.maximum(qi + 1 - nkeep_ref[hd], 0)
    n_off = qi - first

    @pl.when(n_off % 2 == 1)
    def _():
        accumulate(qi - 1, scores(qi - 1))

    def pair(i, carry):
        ka = first + 2 * i
        s_a = scores(ka)
        s_b = scores(ka + 1)
        accumulate(ka, s_a)
        accumulate(ka + 1, s_b)
        return carry

    lax.fori_loop(0, n_off // 2, pair, 0)
    a0 = acc_sc[0]
    a1 = acc_sc[1]
    o = a0[:, 0:DV_A] / a0[:, DV_A:] - lam * (a1[:, 0:DV_A] / a1[:, DV_A:])
    on = o * lax.rsqrt(jnp.mean(o * o, axis=-1, keepdims=True) + EPS) * sub_ref[...]
    o_ref[...] = on * post_scale


def _attn_shift_kernel(par_ref, slope_ref, nkeep_ref, q_ref, qf_ref, k_ref, v_ref, kf_ref, sub_ref, o_ref, acc_sc,
                       *, blk, post_scale):
    _attn_shift_body(pl.program_id(1), pl.program_id(2), par_ref, slope_ref, nkeep_ref, q_ref, qf_ref,
                     k_ref, v_ref, kf_ref, sub_ref, o_ref, acc_sc, blk, post_scale)


def _attn_fused_kernel(pt_ref, par_ref, slope_ref, nkeep_ref, q_ref, qf_ref, k_ref, v_ref, kf_ref, sub_ref,
                       dslope_ref, dq_ref, dkn_ref, dvn_ref, *rest, blk, nq, post_scale, n_pages):
    del pt_ref
    step = pl.program_id(0)
    k_pages = rest[:n_pages]
    v_pages = rest[n_pages:2 * n_pages]
    o_ref, do_ref, acc_sc, s_sc = rest[2 * n_pages:]

    finish = _paged_attn_begin(dslope_ref[...], dq_ref[0], dkn_ref[0], dvn_ref[0], k_pages, s_sc,
                               n_pages)

    def decode_finish():
        do_ref[0] = finish(par_ref[0], sub_ref[...], v_pages, post_scale)

    _attn_shift_body((step // nq) % H_A, step % nq, par_ref, slope_ref, nkeep_ref, q_ref, qf_ref,
                     k_ref, v_ref, kf_ref, sub_ref, o_ref, acc_sc, blk, post_scale,
                     between_diagonal=decode_finish)


def _attn_prompt_specs(nq, seq, blk, index):
    smem = pl.BlockSpec(memory_space=pltpu.SMEM)
    return [
        smem, smem, smem,
        pl.BlockSpec((blk, DV_A), index(lambda b, h, i: (b * nq + i, h))),
        pl.BlockSpec((None, blk, DV_A), index(lambda b, h, i: (h, 0, 0))),
        pl.BlockSpec((seq, DV_A), index(lambda b, h, i: (b, h))),
        pl.BlockSpec((seq, DV_A), index(lambda b, h, i: (b, h))),
        pl.BlockSpec((blk, DV_A), index(lambda b, h, i: (0, 0))),
        pl.BlockSpec((1, DV_A), index(lambda b, h, i: (0, 0))),
    ]


def _attn_prompt(par, slopes, nkeep, q, qf, k, v, kf, sub, nb, seq, blk, post_scale, shifted):
    t = nb * seq
    nq = seq // blk
    if shifted:
        body = functools.partial(_attn_shift_kernel, blk=blk, post_scale=post_scale)
        scratch = [pltpu.VMEM((2, blk, 2 * DV_A), F32)]
        name = "attn_prompt_shift"
    else:
        body = functools.partial(_attn_kernel, blk=blk, post_scale=post_scale)
        scratch = [pltpu.VMEM((2, blk, 1), F32), pltpu.VMEM((2, blk, 1), F32),
                   pltpu.VMEM((2, blk, DV_A), F32)]
        name = "attn_prompt_online"
    return pl.pallas_call(
        body,
        grid=(nb, H_A, nq),
        in_specs=_attn_prompt_specs(nq, seq, blk, lambda f: f),
        out_specs=pl.BlockSpec((blk, DV_A), lambda b, h, i: (b * nq + i, h)),
        out_shape=jax.ShapeDtypeStruct((t, W_A), F32),
        scratch_shapes=scratch,
        name=name,
        compiler_params=pltpu.CompilerParams(
            dimension_semantics=("parallel", "parallel", "arbitrary"),
            vmem_limit_bytes=VMEM_LIMIT),
    )(par, slopes, nkeep, q, qf, k, v, kf, sub)


def _attn_fused(pt_flat, par, slopes, nkeep, q, qf, k, v, kf, sub, slope_rows, dq, dkn, dvn, ck, cv,
                layer, n_pages, nb, seq, blk, post_scale):
    t = nb * seq
    nq = seq // blk
    db = dq.shape[0]
    rows = PAGE_SIZE * H_A

    def with_pt(f):
        return lambda n, pt: f(n // (H_A * nq), (n // nq) % H_A, n % nq)

    def seq_index(n, pt):
        return (n, 0, 0)

    def page_spec(j):
        return pl.BlockSpec((None, None, rows, DV_A),
                            lambda n, pt: (layer, pt[n * n_pages + j], 0, 0))

    in_specs = _attn_prompt_specs(nq, seq, blk, with_pt)
    in_specs += [pl.BlockSpec((ATT_ROWS, 1), lambda n, pt: (0, 0)),
                 pl.BlockSpec((1, 1, QK_A), seq_index),
                 pl.BlockSpec((1, 1, QK_A), seq_index),
                 pl.BlockSpec((1, 1, W_A), seq_index)]
    in_specs += [page_spec(j) for j in range(n_pages)] + [page_spec(j) for j in range(n_pages)]
    grid_spec = pltpu.PrefetchScalarGridSpec(
        num_scalar_prefetch=1,
        grid=(nb * H_A * nq,),
        in_specs=in_specs,
        out_specs=[pl.BlockSpec((blk, DV_A), with_pt(lambda b, h, i: (b * nq + i, h))),
                   pl.BlockSpec((1, H_A, DV_A), seq_index)],
        scratch_shapes=[pltpu.VMEM((2, blk, 2 * DV_A), F32),
                        pltpu.VMEM((ATT_ROWS, n_pages * rows), F32)],
    )
    return pl.pallas_call(
        functools.partial(_attn_fused_kernel, blk=blk, nq=nq, post_scale=post_scale, n_pages=n_pages),
        grid_spec=grid_spec,
        out_shape=[jax.ShapeDtypeStruct((t, W_A), F32), jax.ShapeDtypeStruct((db, H_A, DV_A), F32)],
        name="attn_fused",
        compiler_params=pltpu.CompilerParams(
            dimension_semantics=("arbitrary",),
            vmem_limit_bytes=VMEM_LIMIT),
    )(pt_flat, par, slopes, nkeep, q, qf, k, v, kf, sub, slope_rows,
      dq.reshape(db, 1, QK_A), dkn.reshape(db, 1, QK_A), dvn.reshape(db, 1, W_A),
      *([ck] * n_pages), *([cv] * n_pages))


def _block_diag_rows(m):
    head = lax.broadcasted_iota(jnp.int32, m.shape, 1) // DK_C
    parts = [jnp.where(head == hp, m, 0.0).astype(BF16) for hp in range(H_C)]
    return jnp.concatenate(parts, axis=0)


def _mix_kernel(x_ref, oa_ref, u_ref, g_ref, kk_ref, vc_ref, qh_ref, sg_ref,
                ltri_ref, hh_ref, pw_ref, ps_ref, hg_ref, wo_ref,
                xo_ref, pool_ref, hst_ref,
                ubuf, kkbuf, gbuf, vcbuf, st_sc, inter_sc, xab_sc, *, tc):
    si = pl.program_id(1)
    ns = pl.num_programs(1)
    pad = HGRN_SUB
    nchunk = tc // HGRN_CHUNK

    @pl.when(si == 0)
    def _():
        st_sc[...] = jnp.zeros(st_sc.shape, F32)
        z = jnp.zeros((pad, W_B), F32)
        ubuf[0:pad, :] = z
        kkbuf[0:pad, :] = z
        gbuf[0:pad, :] = z
        vcbuf[0:pad, :] = z

    u = u_ref[...]
    ubuf[pad:pad + tc, :] = u
    acc = ubuf[...]
    wsum = {}
    for k, win in enumerate(POOL_WINDOWS):
        acc = acc + pltpu.roll(acc, 2 ** k, 0)
        wsum[win] = acc[pad:, :]
    pos = (si * tc + lax.broadcasted_iota(jnp.int32, (tc, 1), 0)).astype(F32)
    lane_b = lax.broadcasted_iota(jnp.int32, (tc, W_B), 1)
    pooled = None
    for gi, win in reversed(list(enumerate(POOL_WINDOWS))):
        term = wsum[win] * (1.0 / jnp.minimum(pos + 1.0, float(win)))
        pooled = term if pooled is None else jnp.where(lane_b < (gi + 1) * C_B, term, pooled)
    pooled = pooled - u
    ob = _dot(pooled.astype(BF16), pw_ref[...]) * ps_ref[...]
    ubuf[0:pad, :] = ubuf[tc:tc + pad, :]

    x_ab = (x_ref[...]
            + _dot((oa_ref[...] * sg_ref[:, 0:W_A]).astype(BF16), wo_ref[0:W_A, :])
            + _dot((ob * sg_ref[:, W_A:W_A + W_B]).astype(BF16), wo_ref[W_A:W_A + W_B, :]))
    xab_sc[...] = x_ab

    g = g_ref[...]
    kk = kk_ref[...]
    vc = vc_ref[...]
    qh = qh_ref[...]
    ltri = ltri_ref[...]
    ghi, gmid, glo = _split3(g)
    gc = _dot(ltri, ghi) + _dot(ltri, gmid) + _dot(ltri, glo)
    gc3 = gc.reshape(nchunk, HGRN_CHUNK, W_C)

    def chunk_row(idx):
        r = jnp.broadcast_to(gc3[:, idx:idx + 1, :], (nchunk, HGRN_CHUNK, W_C))
        return r.reshape(tc, W_C)

    rc = lax.broadcasted_iota(jnp.int32, (tc, 1), 0) % HGRN_CHUNK
    hh = hh_ref[...]
    t_idx = lax.broadcasted_iota(jnp.int32, (HGRN_CHUNK, H_C * HGRN_CHUNK), 0)
    s_idx = lax.broadcasted_iota(jnp.int32, (HGRN_CHUNK, H_C * HGRN_CHUNK), 1) % HGRN_CHUNK
    chunks = [slice(c * HGRN_CHUNK, (c + 1) * HGRN_CHUNK) for c in range(nchunk)]

    def hgrn_projection(o):
        ss = _dot((o * o).astype(BF16), hh)
        ocn = o * lax.rsqrt(ss * (1.0 / DV_C) + EPS) * hg_ref[...]
        return _dot((ocn * sg_ref[:, W_A + W_B:]).astype(BF16), wo_ref[W_A + W_B:, :])

    rmid = chunk_row(HGRN_CHUNK // 2 - 1)
    eg = jnp.exp(gc)
    qe = qh * eg
    kdec = kk * jnp.exp(chunk_row(HGRN_CHUNK - 1) - gc)
    qt = qh * jnp.exp(gc - rmid)
    kt = kk * jnp.exp(rmid - gc)
    bdmask = (lax.broadcasted_iota(jnp.int32, (W_C, W_C), 0) // DK_C
              == lax.broadcasted_iota(jnp.int32, (W_C, W_C), 1) // DK_C)
    outs = []
    for c, sl in enumerate(chunks):
        a = _dot_nt(qt[sl].astype(BF16), _block_diag_rows(kt[sl]))
        a = jnp.where(t_idx >= s_idx, a, 0.0)
        st = st_sc[...]
        inter = _dot_nt(qe[sl].astype(BF16), st.astype(BF16))
        inter_sc[sl, :] = inter
        outs.append(_dot(a.astype(BF16), _block_diag_rows(vc[sl])) + inter)
        last = c * HGRN_CHUNK + HGRN_CHUNK - 1
        upd = _dot_tn(vc[sl].astype(BF16), kdec[sl].astype(BF16))
        st_sc[...] = st * eg[last:last + 1, :] + jnp.where(bdmask, upd, 0.0)
    xo_ref[...] = x_ab + hgrn_projection(jnp.concatenate(outs, axis=0))

    safe = jnp.max(jnp.abs(gc - rmid)) <= HGRN_SAFE_EXP

    @pl.when(jnp.logical_not(safe))
    def _():
        upper = rc >= 32
        r31 = chunk_row(31)
        qt1 = jnp.where(upper, qh * jnp.exp(jnp.minimum(gc - r31, 0.0)), 0.0)
        kt1 = jnp.where(upper, 0.0, kk * jnp.exp(jnp.minimum(r31 - gc, 0.0)))
        ref2 = jnp.where(upper, chunk_row(47), chunk_row(15))
        odd = ((rc // HGRN_SUB) % 2) == 1
        qt2 = jnp.where(odd, qh * jnp.exp(jnp.minimum(gc - ref2, 0.0)), 0.0)
        kt2 = jnp.where(odd, 0.0, kk * jnp.exp(jnp.minimum(ref2 - gc, 0.0)))
        tb = t_idx // HGRN_SUB
        mask2 = ((tb % 2) == 1) & (s_idx // HGRN_SUB == tb - 1)

        kkbuf[pad:pad + tc, :] = kk
        gbuf[pad:pad + tc, :] = gc
        vcbuf[pad:pad + tc, :] = vc
        r16 = rc % HGRN_SUB
        od = jnp.zeros((tc, W_C), F32)
        for d in range(HGRN_SUB):
            kks = kkbuf[pad - d:pad - d + tc, :]
            gs = gbuf[pad - d:pad - d + tc, :]
            vcs = vcbuf[pad - d:pad - d + tc, :]
            xd = jnp.where(r16 >= d, qh * kks * jnp.exp(jnp.minimum(gc - gs, 0.0)), 0.0)
            od = od + _dot(xd.astype(BF16), hh) * vcs
        redo = []
        for sl in chunks:
            a1 = _dot_nt(qt1[sl].astype(BF16), _block_diag_rows(kt1[sl]))
            a2 = _dot_nt(qt2[sl].astype(BF16), _block_diag_rows(kt2[sl]))
            a_off = a1 + jnp.where(mask2, a2, 0.0)
            redo.append(_dot(a_off.astype(BF16), _block_diag_rows(vc[sl])) + od[sl] + inter_sc[sl, :])
        xo_ref[...] = xab_sc[...] + hgrn_projection(jnp.concatenate(redo, axis=0))

    @pl.when(si == ns - 1)
    def _():
        pool_ref[0] = u[tc - POOL_BUF:, :]
        hst_ref[0] = st_sc[...]


def _mix_prompt(x, oa, u, g, kk, vc, qh, sg, ltri, hh, pw, ps, hg, wo, layer, nb, seq, tc):
    t = nb * seq
    ns = seq // tc
    row = lambda b, s: (b * ns + s, 0)
    fix = lambda b, s: (0, 0)
    lay = lambda b, s: (layer, 0, 0)
    in_specs = [pl.BlockSpec((tc, D_MODEL), row), pl.BlockSpec((tc, W_A), row)]
    in_specs += [pl.BlockSpec((tc, W_C), row) for _ in range(5)]
    in_specs += [
        pl.BlockSpec((tc, D_MODEL), row),
        pl.BlockSpec((tc, tc), fix),
        pl.BlockSpec((W_C, W_C), fix),
        pl.BlockSpec((None, W_B, W_B), lay),
        pl.BlockSpec((1, W_B), fix),
        pl.BlockSpec((1, W_C), fix),
        pl.BlockSpec((None, D_MODEL, D_MODEL), lay),
    ]
    return pl.pallas_call(
        functools.partial(_mix_kernel, tc=tc),
        grid=(nb, ns),
        in_specs=in_specs,
        out_specs=[
            pl.BlockSpec((tc, D_MODEL), row),
            pl.BlockSpec((1, POOL_BUF, W_B), lambda b, s: (b, 0, 0)),
            pl.BlockSpec((1, W_C, W_C), lambda b, s: (b, 0, 0)),
        ],
        out_shape=[
            jax.ShapeDtypeStruct((t, D_MODEL), F32),
            jax.ShapeDtypeStruct((nb, POOL_BUF, W_B), F32),
            jax.ShapeDtypeStruct((nb, W_C, W_C), F32),
        ],
        scratch_shapes=[
            pltpu.VMEM((tc + HGRN_SUB, W_B), F32),
            pltpu.VMEM((tc + HGRN_SUB, W_C), F32),
            pltpu.VMEM((tc + HGRN_SUB, W_C), F32),
            pltpu.VMEM((tc + HGRN_SUB, W_C), F32),
            pltpu.VMEM((W_C, W_C), F32),
            pltpu.VMEM((tc, W_C), F32),
            pltpu.VMEM((tc, D_MODEL), F32),
        ],
        name="mix_prompt",
        compiler_params=pltpu.CompilerParams(
            dimension_semantics=("parallel", "arbitrary"), vmem_limit_bytes=VMEM_LIMIT),
    )(x, oa, u, g, kk, vc, qh, sg, ltri, hh, pw, ps, hg, wo)


ATT_ROWS = 2 * H_A


def _paged_attn_begin(slope, q_row, kn_row, vn_row, k_refs, s_sc, n_pages):
    rows = PAGE_SIZE * H_A
    past = n_pages * PAGE_SIZE

    r = lax.broadcasted_iota(jnp.int32, (ATT_ROWS, DV_A), 0)
    lane = lax.broadcasted_iota(jnp.int32, (ATT_ROWS, DV_A), 1)

    def head_rows(row):
        out = jnp.zeros((ATT_ROWS, DV_A), F32)
        for h in range(H_A):
            piece = jnp.broadcast_to(row[:, h * DV_A:(h + 1) * DV_A], (ATT_ROWS, DV_A))
            out = jnp.where(r % H_A == h, piece, out)
        return out

    q = jnp.where(lane // DK_A == r // (ATT_ROWS // 2), head_rows(q_row), 0.0)
    kn = head_rows(kn_row)
    vn = head_rows(vn_row)
    qb = q.astype(BF16)

    col = lax.broadcasted_iota(jnp.int32, (ATT_ROWS, rows), 1)
    rr = lax.broadcasted_iota(jnp.int32, (ATT_ROWS, rows), 0)
    valid = (col % H_A) == (rr % H_A)
    tok = col // H_A
    for j in range(n_pages):
        s = _dot_nt(qb, k_refs[j][...].astype(BF16))
        dist = (past - j * PAGE_SIZE - tok).astype(F32)
        s_sc[:, j * rows:(j + 1) * rows] = jnp.where(valid, s - slope * dist, NEG)
    s_self = jnp.sum(q * kn, axis=-1, keepdims=True)

    def finish(lam, sub, v_refs, post_scale):
        s_all = s_sc[...]
        m = jnp.maximum(jnp.max(s_all, axis=-1, keepdims=True), s_self)
        p_self = jnp.exp(s_self - m)
        l = p_self
        acc = p_self * vn
        for j in range(n_pages):
            p = jnp.exp(s_all[:, j * rows:(j + 1) * rows] - m)
            l = l + jnp.sum(p, axis=-1, keepdims=True)
            acc = acc + _dot(p.astype(BF16), v_refs[j][...].astype(BF16))
        o_maps = acc / l
        half = ATT_ROWS // 2
        o = o_maps[0:half] - lam * o_maps[half:ATT_ROWS]
        on = o * lax.rsqrt(jnp.mean(o * o, axis=-1, keepdims=True) + EPS) * sub
        return (on * post_scale)[0:H_A]

    return finish


def _paged_attn_kernel(pt_ref, lam_ref, slope_ref, q_ref, kn_ref, vn_ref, sub_ref, *rest,
                       n_pages, post_scale):
    del pt_ref
    k_refs = rest[:n_pages]
    v_refs = rest[n_pages:2 * n_pages]
    o_ref = rest[2 * n_pages]
    s_sc = rest[2 * n_pages + 1]
    finish = _paged_attn_begin(slope_ref[...], q_ref[0], kn_ref[0], vn_ref[0], k_refs, s_sc, n_pages)
    o_ref[0] = finish(lam_ref[0], sub_ref[...], v_refs, post_scale)


def _attn_paged(pt_flat, lam, slope_rows, q, kn, vn, sub, ck, cv, layer, n_pages, post_scale):
    nb = q.shape[0]
    rows = PAGE_SIZE * H_A
    smem = pl.BlockSpec(memory_space=pltpu.SMEM)
    row3 = lambda b, pt: (b, 0, 0)

    def page_spec(j):
        return pl.BlockSpec((None, None, rows, DV_A),
                            lambda b, pt, j=j: (layer, pt[b * n_pages + j], 0, 0))

    in_specs = [smem,
                pl.BlockSpec((ATT_ROWS, 1), lambda b, pt: (0, 0)),
                pl.BlockSpec((1, 1, QK_A), row3),
                pl.BlockSpec((1, 1, QK_A), row3),
                pl.BlockSpec((1, 1, W_A), row3),
                pl.BlockSpec((1, DV_A), lambda b, pt: (0, 0))]
    in_specs += [page_spec(j) for j in range(n_pages)] * 2
    grid_spec = pltpu.PrefetchScalarGridSpec(
        num_scalar_prefetch=1,
        grid=(nb,),
        in_specs=in_specs,
        out_specs=pl.BlockSpec((1, H_A, DV_A), row3),
        scratch_shapes=[pltpu.VMEM((ATT_ROWS, n_pages * rows), F32)],
    )
    return pl.pallas_call(
        functools.partial(_paged_attn_kernel, n_pages=n_pages, post_scale=post_scale),
        grid_spec=grid_spec,
        out_shape=jax.ShapeDtypeStruct((nb, H_A, DV_A), F32),
        name="attn_paged",
        compiler_params=pltpu.CompilerParams(
            dimension_semantics=("arbitrary",), vmem_limit_bytes=VMEM_LIMIT),
    )(pt_flat, lam, slope_rows, q.reshape(nb, 1, QK_A), kn.reshape(nb, 1, QK_A),
      vn.reshape(nb, 1, W_A), sub, *([ck] * n_pages), *([cv] * n_pages))


def _dec_mix_kernel(x_ref, oa_ref, u_ref, gt_ref, kkt_ref, vct_ref, qht_ref, sg_ref, sp_ref, sh_ref,
                    pall_ref, hall_ref, pw_ref, ps_ref, hgc_ref, wo_ref,
                    xo_ref, po_ref, ho_ref, ot_sc, *, past):
    del pall_ref, hall_ref
    h = pl.program_id(0)
    nb = x_ref.shape[0]

    @pl.when(h == 0)
    def _():
        u = u_ref[...]
        lane_b = lax.broadcasted_iota(jnp.int32, (nb, W_B), 1)
        acc = u
        wsum = {}
        for j in range(1, max(POOL_WINDOWS)):
            acc = acc + sp_ref[POOL_BUF - j]
            if j + 1 in POOL_WINDOWS:
                wsum[j + 1] = acc
        pooled = None
        for gi, win in reversed(list(enumerate(POOL_WINDOWS))):
            term = wsum[win] * (1.0 / min(past + 1, win))
            pooled = term if pooled is None else jnp.where(lane_b < (gi + 1) * C_B, term, pooled)
        pooled = pooled - u
        for j in range(POOL_BUF - 1):
            po_ref[j] = sp_ref[j + 1]
        po_ref[POOL_BUF - 1] = u
        ob = _dot(pooled.astype(BF16), pw_ref[...]) * ps_ref[...]
        xo_ref[...] = (x_ref[...]
                       + _dot((oa_ref[...] * sg_ref[:, 0:W_A]).astype(BF16), wo_ref[0:W_A, :])
                       + _dot((ob * sg_ref[:, W_A:W_A + W_B]).astype(BF16), wo_ref[W_A:W_A + W_B, :]))

    base = pl.multiple_of(h * DK_C, DK_C)
    vt = vct_ref[pl.ds(base, DV_C), :]

    def per_d(d, acc):
        r = base + d
        s_new = jnp.exp(gt_ref[pl.ds(r, 1), :]) * sh_ref[d] + kkt_ref[pl.ds(r, 1), :] * vt
        ho_ref[d] = s_new
        return acc + qht_ref[pl.ds(r, 1), :] * s_new

    ot_sc[pl.ds(base, DV_C), :] = lax.fori_loop(0, DK_C, per_d, jnp.zeros((DV_C, nb), F32))

    @pl.when(h == pl.num_programs(0) - 1)
    def _():
        ot = ot_sc[...].reshape(H_C, DV_C, nb)
        otn = ot * lax.rsqrt(jnp.mean(ot * ot, axis=1, keepdims=True) + EPS)
        mix_t = otn.reshape(W_C, nb) * hgc_ref[...] * sg_ref[:, W_A + W_B:].T
        xo_ref[...] += _dot_tn(mix_t.astype(BF16), wo_ref[W_A + W_B:, :])


def _mix_decode(x, oa, u, gt, kkt, vct, qht, sg, sp, sh, pool_all, hgrn_all, pw, ps, hgc, wo,
                layer, past):
    nb = x.shape[0]
    fix = lambda h: (0, 0)
    pool_spec = pl.BlockSpec((None, POOL_BUF, nb, W_B), lambda h: (layer, 0, 0, 0))
    state_spec = pl.BlockSpec((None, None, DK_C, DV_C, nb), lambda h: (layer, h, 0, 0, 0))
    any_spec = pl.BlockSpec(memory_space=pl.ANY)
    in_specs = [
        pl.BlockSpec((nb, D_MODEL), fix),
        pl.BlockSpec((nb, W_A), fix),
        pl.BlockSpec((nb, W_B), fix),
        pl.BlockSpec((W_C, nb), fix),
        pl.BlockSpec((W_C, nb), fix),
        pl.BlockSpec((W_C, nb), fix),
        pl.BlockSpec((W_C, nb), fix),
        pl.BlockSpec((nb, D_MODEL), fix),
        pool_spec,
        state_spec,
        any_spec,
        any_spec,
        pl.BlockSpec((None, W_B, W_B), lambda h: (layer, 0, 0)),
        pl.BlockSpec((1, W_B), fix),
        pl.BlockSpec((None, W_C, 1), lambda h: (layer, 0, 0)),
        pl.BlockSpec((None, D_MODEL, D_MODEL), lambda h: (layer, 0, 0)),
    ]
    return pl.pallas_call(
        functools.partial(_dec_mix_kernel, past=past),
        grid=(H_C,),
        in_specs=in_specs,
        out_specs=[pl.BlockSpec((nb, D_MODEL), fix), pool_spec, state_spec],
        out_shape=[
            jax.ShapeDtypeStruct((nb, D_MODEL), F32),
            jax.ShapeDtypeStruct(pool_all.shape, F32),
            jax.ShapeDtypeStruct(hgrn_all.shape, F32),
        ],
        scratch_shapes=[pltpu.VMEM((W_C, nb), F32)],
        input_output_aliases={10: 1, 11: 2},
        name="mix_decode",
        compiler_params=pltpu.CompilerParams(
            dimension_semantics=("arbitrary",), vmem_limit_bytes=VMEM_LIMIT),
    )(x, oa, u, gt, kkt, vct, qht, sg, sp, sh, pool_all, hgrn_all, pw, ps, hgc, wo)


def _block_ones(n, blk):
    idx = np.arange(n) // blk
    return jnp.asarray((idx[:, None] == idx[None, :]).astype(np.float32), dtype=BF16)


def _chunk_lower_tri(n, blk):
    idx = np.arange(n)
    same = (idx[:, None] // blk) == (idx[None, :] // blk)
    return jnp.asarray((same & (idx[:, None] >= idx[None, :])).astype(np.float32), dtype=BF16)


def kernel(x_prompt, x_sample, cache_k, cache_v, state_pool, state_hgrn, page_table,
           ln_gain, w_in, qn_gain, kn_gain, lam_q1, lam_k1, lam_q2, lam_k2, subln_gain,
           pool_w, pool_scale, hgrn_lb, hgrn_norm_gain, w_out):
    nb, seq, _ = x_prompt.shape
    t = nb * seq
    db = x_sample.shape[0]
    n_pages = page_table.shape[1]
    past = n_pages * PAGE_SIZE
    n_phys = cache_k.shape[1]

    tm = min(512, t)
    blk = min(512, seq)
    tc = min(256, seq)

    w_in_b = w_in.astype(BF16)
    w_out_b = w_out.astype(BF16)
    sm = jax.nn.softmax(hgrn_lb.astype(F32), axis=0)
    lb_all = jnp.cumsum(sm, axis=0) - sm[0]
    lam_init = [0.8 - 0.6 * math.exp(-0.3 * l) for l in range(DEPTH)]
    lam_all = (jnp.exp(jnp.sum(lam_q1.astype(F32) * lam_k1.astype(F32), axis=-1))
               - jnp.exp(jnp.sum(lam_q2.astype(F32) * lam_k2.astype(F32), axis=-1))
               + jnp.asarray(lam_init, F32))
    slopes_np = np.asarray([2.0 ** (-8.0 * (h + 1) / H_A) for h in range(H_A)], np.float32)
    slopes = jnp.asarray(slopes_np)
    slope_rows = jnp.asarray(slopes_np[np.arange(ATT_ROWS) % H_A].reshape(ATT_ROWS, 1))
    score_bound = (1.0125 * DK_A ** 0.5) * jnp.max(
        jnp.abs(qn_gain.astype(F32) * kn_gain.astype(F32)), axis=-1)
    sb_hi = score_bound.astype(BF16).astype(F32)
    sb_mid = (score_bound - sb_hi).astype(BF16).astype(F32)
    sb_lo = (score_bound - sb_hi - sb_mid).astype(BF16).astype(F32)
    attn_par = jnp.stack([lam_all, sb_hi, sb_mid, sb_lo], axis=1)
    kfeat = _attn_k_features(blk)
    qfeat = [_attn_q_features(blk, slopes_np, attn_par[l, 1:]) for l in range(DEPTH)]
    nkeep = _attn_blocks_kept(blk, seq // blk, slopes_np)
    fuse_decode = nb * H_A * (seq // blk) == db
    qg = jnp.tile(qn_gain.astype(F32), (1, QK_A // DK_A))
    kg = jnp.tile(kn_gain.astype(F32), (1, QK_A // DK_A))
    hg_tiled = jnp.tile(hgrn_norm_gain.astype(F32), (1, H_C))
    eye_g = jnp.eye(G_B, dtype=F32)
    pw_bd = jnp.einsum('lgcd,gh->lgchd', pool_w.astype(F32), eye_g).reshape(DEPTH, W_B, W_B).astype(BF16)
    hh = _block_ones(W_C, DV_C)
    ltri = _chunk_lower_tri(tc, HGRN_CHUNK)
    ck = cache_k.reshape(DEPTH, n_phys, PAGE_SIZE * H_A, 2 * DK_A)
    cv = cache_v.reshape(DEPTH, n_phys, PAGE_SIZE * H_A, DV_A)
    pt_flat = page_table.reshape(-1).astype(jnp.int32)

    xp = x_prompt.reshape(t, D_MODEL)
    xs = x_sample.reshape(db, D_MODEL)
    k_all = jnp.zeros((DEPTH, t * H_A, 2 * DK_A), F32)
    v_all = jnp.zeros((DEPTH, t * H_A, DV_A), F32)
    pools, states = [], []
    dks, dvs = [], []
    sp_in = jnp.transpose(state_pool, (0, 2, 1, 3))
    sh_in = jnp.transpose(state_hgrn, (0, 2, 3, 4, 1))
    dpool_all = jnp.zeros(sp_in.shape, F32)
    dstate_all = jnp.zeros(sh_in.shape, F32)
    hg_col = jnp.tile(hgrn_norm_gain.astype(F32), (1, H_C)).reshape(DEPTH, W_C, 1)
    for l in range(DEPTH):
        lng = ln_gain[l].reshape(1, D_MODEL).astype(F32)
        lb = lb_all[l].reshape(1, W_C)
        sub = subln_gain[l].reshape(1, DV_A).astype(F32)
        ps = pool_scale[l].reshape(1, W_B).astype(F32)
        post = 1.0 - lam_init[l]

        q, k_all, kb, v_all, vb, u, g, kk, vc, qh, sg = _inproj_prompt(
            xp, lng, w_in_b, qg[l:l + 1], kg[l:l + 1], lb, hh, k_all, v_all, l, tm)
        dq, dk, dv, du, dgt, dkkt, dvct, dqht, dsg = _inproj_decode(
            xs, lng, w_in_b, qg[l:l + 1], kg[l:l + 1], lb, hh, l)

        def attend(par, qf, q, kb, vb, sub, dq, dk, dv, shifted, layer=l, post=post):
            if shifted and fuse_decode:
                return _attn_fused(pt_flat, par, slopes, nkeep, q, qf, kb, vb, kfeat, sub, slope_rows,
                                   dq, dk, dv, ck, cv, layer, n_pages, nb, seq, blk, post)
            oa = _attn_prompt(par, slopes, nkeep, q, qf, kb, vb, kfeat, sub, nb, seq, blk, post, shifted)
            doa = _attn_paged(pt_flat, par[0:1], slope_rows, dq, dk, dv, sub, ck, cv, layer,
                              n_pages, post)
            return oa, doa

        oa, doa = lax.cond(
            score_bound[l] <= ATTN_SAFE_SHIFT,
            functools.partial(attend, shifted=True), functools.partial(attend, shifted=False),
            attn_par[l], qfeat[l], q, kb, vb, sub, dq, dk, dv)
        xp, pool_new, st = _mix_prompt(xp, oa, u, g, kk, vc, qh, sg, ltri, hh, pw_bd, ps,
                                       hg_tiled[l:l + 1], w_out_b, l, nb, seq, tc)
        pools.append(pool_new)
        states.append(st)
        xs, dpool_all, dstate_all = _mix_decode(
            xs, doa.reshape(db, W_A), du, dgt, dkkt, dvct, dqht, dsg, sp_in, sh_in,
            dpool_all, dstate_all, pw_bd, ps, hg_col, w_out_b, l, past)
        dks.append(dk)
        dvs.append(dv)

    y_prompt = xp.reshape(nb, seq, D_MODEL)
    y_sample = xs.reshape(db, 1, D_MODEL)
    k_prompt = k_all.reshape(DEPTH, nb, seq, H_A, 2 * DK_A)
    v_prompt = v_all.reshape(DEPTH, nb, seq, H_A, DV_A)
    k_sample = jnp.stack(dks).reshape(DEPTH, db, 1, H_A, 2 * DK_A)
    v_sample = jnp.stack(dvs).reshape(DEPTH, db, 1, H_A, DV_A)
    pool_prompt = jnp.stack(pools)
    pool_sample = jnp.transpose(dpool_all, (0, 2, 1, 3))
    st_all = jnp.stack(states).reshape(DEPTH, nb, H_C, DV_C, H_C, DK_C)
    hgrn_prompt = jnp.stack([st_all[:, :, h, :, h, :] for h in range(H_C)], axis=2).swapaxes(-1, -2)
    hgrn_sample = jnp.transpose(dstate_all, (0, 4, 1, 2, 3))
    return (y_prompt, y_sample, k_prompt, v_prompt, k_sample, v_sample,
            pool_prompt, pool_sample, hgrn_prompt, hgrn_sample)
```

```python
import functools
import math

import numpy as np
import jax
import jax.numpy as jnp
from jax import lax
from jax.experimental import pallas as pl
from jax.experimental.pallas import tpu as pltpu

F32 = jnp.float32
BF16 = jnp.bfloat16

D_MODEL = 1024
DEPTH = 4
PAGE_SIZE = 128
H_A = 4
DK_A = 64
DV_A = 128
W_A = H_A * DV_A
QK_A = H_A * 2 * DK_A
G_B = 4
POOL_WINDOWS = (2, 4, 8, 16)
W_B = 256
C_B = W_B // G_B
POOL_BUF = max(POOL_WINDOWS) - 1
H_C = 4
W_C = 256
DK_C = 64
DV_C = 64
D_IN = 2 * QK_A + W_A + W_B + 3 * W_C + D_MODEL
HGRN_CHUNK = 64
HGRN_SUB = 16
HGRN_SAFE_EXP = 60.0
ATTN_SAFE_SHIFT = 20.0
EPS = 1e-6
NEG = -1e30
assert POOL_WINDOWS == tuple(2 ** (k + 1) for k in range(len(POOL_WINDOWS)))

C_Q, C_K, C_V, C_U = 0, QK_A, 2 * QK_A, 2 * QK_A + W_A
C_F = C_U + W_B
C_I = C_F + W_C
C_QC = C_I + W_C
C_GATE = C_QC + W_C

V7X_VMEM_BYTES = 64 * 1024 * 1024
VMEM_LIMIT = 56 * 1024 * 1024


def _dot(a, b):
    return jnp.dot(a, b, preferred_element_type=F32)


def _dot_nt(a, b):
    return lax.dot_general(a, b, (((1,), (1,)), ((), ())), preferred_element_type=F32)


def _dot_tn(a, b):
    return lax.dot_general(a, b, (((0,), (0,)), ((), ())), preferred_element_type=F32)


def _split3(x):
    hi = x.astype(BF16)
    r = x - hi.astype(F32)
    mid = r.astype(BF16)
    lo = (r - mid.astype(F32)).astype(BF16)
    return hi, mid, lo


def _sigmoid(x):
    return 1.0 / (1.0 + jnp.exp(-x))


def _inproj_body(x_ref, lng_ref, w_ref, qg_ref, kg_ref, lb_ref, bd_ref, channel_major_vc=False):
    x = x_ref[...]
    h = x * lax.rsqrt(jnp.mean(x * x, axis=-1, keepdims=True) + EPS) * lng_ref[...]
    hb = h.astype(BF16)
    bd = bd_ref[...]

    def proj(lo, hi):
        return _dot(hb, w_ref[:, lo:hi])

    def head_norm(z, gain):
        sq = (z * z).astype(BF16)
        bw = bd.shape[0]
        ss = jnp.concatenate([_dot(sq[:, j:j + bw], bd) for j in range(0, z.shape[1], bw)], axis=1)
        return z * lax.rsqrt(ss * (1.0 / DK_A) + EPS) * gain

    qn = head_norm(proj(C_Q, C_K), qg_ref[...]) * (DK_A ** -0.5)
    kn = head_norm(proj(C_K, C_V), kg_ref[...])
    v = proj(C_V, C_U)
    u = proj(C_U, C_F)
    lb = lb_ref[...]
    sig = _sigmoid(proj(C_F, C_I))
    g = jnp.log(lb + (1.0 - lb) * sig)
    kk = (1.0 - lb) * (1.0 - sig)
    if channel_major_vc:
        vc = lax.dot_general(w_ref[:, C_I:C_QC], hb, (((0,), (1,)), ((), ())),
                             preferred_element_type=F32)
    else:
        vc = proj(C_I, C_QC)
    qc = proj(C_QC, C_GATE)
    qh = qc * _sigmoid(qc)
    gate = proj(C_GATE, D_IN)
    sg = gate * _sigmoid(gate)
    return qn, kn, v, u, g, kk, vc, qh, sg


def _inproj_prompt_kernel(x_ref, lng_ref, w_ref, qg_ref, kg_ref, lb_ref, bd_ref, kall_ref, vall_ref,
                          q_ref, k_ref, kb_ref, v_ref, vb_ref, u_ref, g_ref, kk_ref, vc_ref,
                          qh_ref, sg_ref):
    del kall_ref, vall_ref
    qn, kn, v, u, g, kk, vc, qh, sg = _inproj_body(x_ref, lng_ref, w_ref, qg_ref, kg_ref, lb_ref, bd_ref)
    tm = x_ref.shape[0]
    q_ref[...] = qn.astype(BF16)
    for h in range(H_A):
        k_ref[pl.ds(h, tm, stride=H_A), :] = kn[:, h * DV_A:(h + 1) * DV_A]
        v_ref[pl.ds(h, tm, stride=H_A), :] = v[:, h * DV_A:(h + 1) * DV_A]
    kb_ref[...] = kn.astype(BF16)
    vb_ref[...] = v.astype(BF16)
    u_ref[...] = u
    g_ref[...] = g
    kk_ref[...] = kk
    vc_ref[...] = vc
    qh_ref[...] = qh
    sg_ref[...] = sg


def _inproj_decode_kernel(x_ref, lng_ref, w_ref, qg_ref, kg_ref, lb_ref, bd_ref,
                          q_ref, k_ref, v_ref, u_ref, gt_ref, kkt_ref, vct_ref, qht_ref, sg_ref):
    qn, kn, v, u, g, kk, vct, qh, sg = _inproj_body(x_ref, lng_ref, w_ref, qg_ref, kg_ref, lb_ref, bd_ref,
                                                    channel_major_vc=True)
    q_ref[...] = qn
    k_ref[...] = kn
    v_ref[...] = v
    u_ref[...] = u
    gt_ref[...] = g.T
    kkt_ref[...] = kk.T
    vct_ref[...] = vct
    qht_ref[...] = qh.T
    sg_ref[...] = sg


def _inproj_in_specs(tm, layer):
    row = lambda i: (i, 0)
    fix = lambda i: (0, 0)
    return [
        pl.BlockSpec((tm, D_MODEL), row),
        pl.BlockSpec((1, D_MODEL), fix),
        pl.BlockSpec((None, D_MODEL, D_IN), lambda i: (layer, 0, 0)),
        pl.BlockSpec((1, QK_A), fix),
        pl.BlockSpec((1, QK_A), fix),
        pl.BlockSpec((1, W_C), fix),
        pl.BlockSpec((W_C, W_C), fix),
    ]


def _inproj_prompt(x, lng, w, qg, kg, lb, bd, k_all, v_all, layer, tm):
    t = x.shape[0]
    row = lambda i: (i, 0)

    def plain(wd, dt):
        return pl.BlockSpec((tm, wd), row), jax.ShapeDtypeStruct((t, wd), dt)

    def stacked(arr):
        return (pl.BlockSpec((None, tm * H_A, DV_A), lambda i: (layer, i, 0)),
                jax.ShapeDtypeStruct(arr.shape, arr.dtype))

    outs = [plain(QK_A, BF16), stacked(k_all), plain(QK_A, BF16), stacked(v_all), plain(W_A, BF16),
            plain(W_B, F32), plain(W_C, F32), plain(W_C, F32), plain(W_C, F32), plain(W_C, F32),
            plain(D_MODEL, F32)]
    n_in = len(_inproj_in_specs(tm, layer))
    any_spec = pl.BlockSpec(memory_space=pl.ANY)
    return pl.pallas_call(
        _inproj_prompt_kernel,
        grid=(t // tm,),
        in_specs=_inproj_in_specs(tm, layer) + [any_spec, any_spec],
        out_specs=[o[0] for o in outs],
        out_shape=[o[1] for o in outs],
        input_output_aliases={n_in: 1, n_in + 1: 3},
        name="inproj_prompt",
        compiler_params=pltpu.CompilerParams(
            dimension_semantics=("parallel",), vmem_limit_bytes=VMEM_LIMIT),
    )(x, lng, w, qg, kg, lb, bd, k_all, v_all)


def _inproj_decode(x, lng, w, qg, kg, lb, bd, layer):
    t = x.shape[0]
    row = lambda i: (i, 0)
    shapes = [(t, QK_A), (t, QK_A), (t, W_A), (t, W_B), (W_C, t), (W_C, t), (W_C, t), (W_C, t),
              (t, D_MODEL)]
    return pl.pallas_call(
        _inproj_decode_kernel,
        grid=(1,),
        in_specs=_inproj_in_specs(t, layer),
        out_specs=[pl.BlockSpec(s, row) for s in shapes],
        out_shape=[jax.ShapeDtypeStruct(s, F32) for s in shapes],
        name="inproj_decode",
        compiler_params=pltpu.CompilerParams(
            dimension_semantics=("arbitrary",), vmem_limit_bytes=VMEM_LIMIT),
    )(x, lng, w, qg, kg, lb, bd)


def _attn_kernel(par_ref, slope_ref, nkeep_ref, q_ref, qf_ref, k_ref, v_ref, kf_ref, sub_ref, o_ref,
                 m_sc, l_sc, acc_sc, *, blk, post_scale):
    del nkeep_ref, qf_ref, kf_ref
    hd = pl.program_id(1)
    qi = pl.program_id(2)
    slope = slope_ref[hd]
    lam = par_ref[0]
    q = q_ref[...]
    lane = lax.broadcasted_iota(jnp.int32, q.shape, 1)
    zero = jnp.zeros_like(q)
    qm = (jnp.where(lane < DK_A, q, zero), jnp.where(lane >= DK_A, q, zero))
    rel = (lax.broadcasted_iota(jnp.int32, (blk, blk), 0)
           - lax.broadcasted_iota(jnp.int32, (blk, blk), 1))
    srel = slope * rel.astype(F32)
    m_sc[...] = jnp.full(m_sc.shape, NEG, F32)
    l_sc[...] = jnp.zeros(l_sc.shape, F32)
    acc_sc[...] = jnp.zeros(acc_sc.shape, F32)

    def step(kj, masked):
        start = pl.multiple_of(kj * blk, blk)
        kb = k_ref[pl.ds(start, blk), :]
        vb = v_ref[pl.ds(start, blk), :]
        off = slope * ((qi - kj) * blk).astype(F32)
        for c in range(2):
            s = _dot_nt(qm[c], kb) - srel - off
            if masked:
                s = jnp.where(rel >= 0, s, NEG)
            m_old = m_sc[c]
            m_new = jnp.maximum(m_old, jnp.max(s, axis=-1, keepdims=True))
            alpha = jnp.exp(m_old - m_new)
            p = jnp.exp(s - m_new)
            l_sc[c] = alpha * l_sc[c] + jnp.sum(p, axis=-1, keepdims=True)
            acc_sc[c] = alpha * acc_sc[c] + _dot(p.astype(BF16), vb)
            m_sc[c] = m_new

    def body(kj, carry):
        step(kj, False)
        return carry

    lax.fori_loop(0, qi, body, 0)
    step(qi, True)
    o = acc_sc[0] / l_sc[0] - lam * (acc_sc[1] / l_sc[1])
    on = o * lax.rsqrt(jnp.mean(o * o, axis=-1, keepdims=True) + EPS) * sub_ref[...]
    o_ref[...] = on * post_scale


N_SHIFT_PARTS = 3
FEAT_SUB = 64


def _attn_k_features(blk):
    j = np.arange(blk)
    f = np.zeros((blk, DV_A), np.float32)
    f[:, 0] = 1.0
    f[:, 1] = 1.0
    f[:, 2] = j // FEAT_SUB
    f[:, 3] = j % FEAT_SUB
    f[:, 4:4 + N_SHIFT_PARTS] = 1.0
    return jnp.asarray(f, dtype=BF16)


ATTN_ZERO_EXP = -104.0


def _attn_blocks_kept(blk, nq, slopes_np):
    kept = []
    for slope in slopes_np:
        n = 1
        while n < nq and slope * ((n - 1) * blk + 1) <= -ATTN_ZERO_EXP:
            n += 1
        kept.append(n)
    return jnp.asarray(kept, jnp.int32)


def _attn_q_features(blk, slopes_np, shift_parts):
    i = np.arange(blk)
    f = np.zeros((H_A, blk, DV_A), np.float32)
    for h in range(H_A):
        f[h, :, 0] = -slopes_np[h] * FEAT_SUB * (i // FEAT_SUB)
        f[h, :, 1] = -slopes_np[h] * (i % FEAT_SUB)
        f[h, :, 2] = slopes_np[h] * FEAT_SUB
        f[h, :, 3] = slopes_np[h]
    lane = np.arange(DV_A)
    out = jnp.asarray(f)
    for part in range(N_SHIFT_PARTS):
        out = jnp.where(lane == 4 + part, -shift_parts[part], out)
    return out.astype(BF16)


def _attn_shift_body(hd, qi, par_ref, slope_ref, nkeep_ref, q_ref, qf_ref, k_ref, v_ref, kf_ref, sub_ref,
                     o_ref, acc_sc, blk, post_scale, between_diagonal=None):
    slope = slope_ref[hd]
    lam = par_ref[0]
    q = q_ref[...]
    lane = lax.broadcasted_iota(jnp.int32, q.shape, 1)
    qfb = qf_ref[...]
    zero = jnp.zeros_like(q)
    qe = (jnp.concatenate([jnp.where(lane < DK_A, q, zero), qfb], axis=1),
          jnp.concatenate([jnp.where(lane >= DK_A, q, zero), qfb], axis=1))
    kf = kf_ref[...]
    ones = jnp.ones((blk, DV_A), BF16)

    def scores(kj):
        start = pl.multiple_of(kj * blk, blk)
        kb = jnp.concatenate([k_ref[pl.ds(start, blk), :], kf], axis=1)
        return [_dot_nt(qe[c], kb) for c in range(2)]

    def accumulate(kj, s):
        start = pl.multiple_of(kj * blk, blk)
        vb = jnp.concatenate([v_ref[pl.ds(start, blk), :], ones], axis=1)
        cstep = slope * ((qi - kj) * blk).astype(F32)
        for c in range(2):
            acc_sc[c] += _dot(jnp.exp(s[c] - cstep).astype(BF16), vb)

    half = blk // 2
    diag_start = pl.multiple_of(qi * blk, blk)
    kb_diag = jnp.concatenate([k_ref[pl.ds(diag_start, blk), :], kf], axis=1)
    s_top = [_dot_nt(qe[c][0:half], kb_diag[0:half]) for c in range(2)]
    s_bot = [_dot_nt(qe[c][half:blk], kb_diag) for c in range(2)]
    if between_diagonal is not None:
        between_diagonal()
    vb_diag = jnp.concatenate([v_ref[pl.ds(diag_start, blk), :], ones], axis=1)
    rel_top = (lax.broadcasted_iota(jnp.int32, (half, half), 0)
               - lax.broadcasted_iota(jnp.int32, (half, half), 1))
    rel_bot = (lax.broadcasted_iota(jnp.int32, (half, blk), 0) + half
               - lax.broadcasted_iota(jnp.int32, (half, blk), 1))
    for c in range(2):
        p_top = jnp.where(rel_top >= 0, jnp.exp(s_top[c]), 0.0).astype(BF16)
        p_bot = jnp.where(rel_bot >= 0, jnp.exp(s_bot[c]), 0.0).astype(BF16)
        acc_sc[c, 0:half, :] = _dot(p_top, vb_diag[0:half])
        acc_sc[c, half:blk, :] = _dot(p_bot, vb_diag)

    first = jnp.maximum(qi + 1 - nkeep_ref[hd], 0)
    n_off = qi - first

    @pl.when(n_off % 2 == 1)
    def _():
        accumulate(qi - 1, scores(qi - 1))

    def pair(i, carry):
        ka = first + 2 * i
        s_a = scores(ka)
        s_b = scores(ka + 1)
        accumulate(ka, s_a)
        accumulate(ka + 1, s_b)
        return carry

    lax.fori_loop(0, n_off // 2, pair, 0)
    a0 = acc_sc[0]
    a1 = acc_sc[1]
    o = a0[:, 0:DV_A] / a0[:, DV_A:] - lam * (a1[:, 0:DV_A] / a1[:, DV_A:])
    on = o * lax.rsqrt(jnp.mean(o * o, axis=-1, keepdims=True) + EPS) * sub_ref[...]
    o_ref[...] = on * post_scale


def _attn_shift_kernel(par_ref, slope_ref, nkeep_ref, q_ref, qf_ref, k_ref, v_ref, kf_ref, sub_ref, o_ref, acc_sc,
                       *, blk, post_scale):
    _attn_shift_body(pl.program_id(1), pl.program_id(2), par_ref, slope_ref, nkeep_ref, q_ref, qf_ref,
                     k_ref, v_ref, kf_ref, sub_ref, o_ref, acc_sc, blk, post_scale)


def _attn_fused_kernel(pt_ref, par_ref, slope_ref, nkeep_ref, q_ref, qf_ref, k_ref, v_ref, kf_ref, sub_ref,
                       dslope_ref, dq_ref, dkn_ref, dvn_ref, ck_hbm, cv_hbm, o_ref, do_ref,
                       acc_sc, s_sc, kbuf, vbuf, sem, *, blk, nq, post_scale, n_pages, layer):
    step = pl.program_id(0)
    n_steps = pl.num_programs(0)
    slot = step % 2

    def page_copy(which, seq, to_slot, j):
        src, dst = (ck_hbm, kbuf) if which == 0 else (cv_hbm, vbuf)
        return pltpu.make_async_copy(src.at[layer, pt_ref[seq * n_pages + j]], dst.at[to_slot, j],
                                     sem.at[to_slot, which])

    def start_pages(seq, to_slot):
        for j in range(n_pages):
            page_copy(0, seq, to_slot, j).start()
        for j in range(n_pages):
            page_copy(1, seq, to_slot, j).start()

    def wait_pages(which):
        for j in range(n_pages):
            page_copy(which, step, slot, j).wait()

    @pl.when(step == 0)
    def _():
        start_pages(step, slot)

    @pl.when(step + 1 < n_steps)
    def _():
        start_pages(step + 1, 1 - slot)

    wait_pages(0)
    k_pages = [kbuf.at[slot, j] for j in range(n_pages)]
    v_pages = [vbuf.at[slot, j] for j in range(n_pages)]

    finish = _paged_attn_begin(dslope_ref[...], dq_ref[0], dkn_ref[0], dvn_ref[0], k_pages, s_sc,
                               n_pages)

    def decode_finish():
        wait_pages(1)
        do_ref[0] = finish(par_ref[0], sub_ref[...], v_pages, post_scale)

    _attn_shift_body((step // nq) % H_A, step % nq, par_ref, slope_ref, nkeep_ref, q_ref, qf_ref,
                     k_ref, v_ref, kf_ref, sub_ref, o_ref, acc_sc, blk, post_scale,
                     between_diagonal=decode_finish)


def _attn_prompt_specs(nq, seq, blk, index):
    smem = pl.BlockSpec(memory_space=pltpu.SMEM)
    return [
        smem, smem, smem,
        pl.BlockSpec((blk, DV_A), index(lambda b, h, i: (b * nq + i, h))),
        pl.BlockSpec((None, blk, DV_A), index(lambda b, h, i: (h, 0, 0))),
        pl.BlockSpec((seq, DV_A), index(lambda b, h, i: (b, h))),
        pl.BlockSpec((seq, DV_A), index(lambda b, h, i: (b, h))),
        pl.BlockSpec((blk, DV_A), index(lambda b, h, i: (0, 0))),
        pl.BlockSpec((1, DV_A), index(lambda b, h, i: (0, 0))),
    ]


def _attn_prompt(par, slopes, nkeep, q, qf, k, v, kf, sub, nb, seq, blk, post_scale, shifted):
    t = nb * seq
    nq = seq // blk
    if shifted:
        body = functools.partial(_attn_shift_kernel, blk=blk, post_scale=post_scale)
        scratch = [pltpu.VMEM((2, blk, 2 * DV_A), F32)]
        name = "attn_prompt_shift"
    else:
        body = functools.partial(_attn_kernel, blk=blk, post_scale=post_scale)
        scratch = [pltpu.VMEM((2, blk, 1), F32), pltpu.VMEM((2, blk, 1), F32),
                   pltpu.VMEM((2, blk, DV_A), F32)]
        name = "attn_prompt_online"
    return pl.pallas_call(
        body,
        grid=(nb, H_A, nq),
        in_specs=_attn_prompt_specs(nq, seq, blk, lambda f: f),
        out_specs=pl.BlockSpec((blk, DV_A), lambda b, h, i: (b * nq + i, h)),
        out_shape=jax.ShapeDtypeStruct((t, W_A), F32),
        scratch_shapes=scratch,
        name=name,
        compiler_params=pltpu.CompilerParams(
            dimension_semantics=("parallel", "parallel", "arbitrary"),
            vmem_limit_bytes=VMEM_LIMIT),
    )(par, slopes, nkeep, q, qf, k, v, kf, sub)


def _attn_fused(pt_flat, par, slopes, nkeep, q, qf, k, v, kf, sub, slope_rows, dq, dkn, dvn, ck, cv,
                layer, n_pages, nb, seq, blk, post_scale):
    t = nb * seq
    nq = seq // blk
    db = dq.shape[0]
    rows = PAGE_SIZE * H_A

    def with_pt(f):
        return lambda n, pt: f(n // (H_A * nq), (n // nq) % H_A, n % nq)

    def seq_index(n, pt):
        return (n, 0, 0)

    any_spec = pl.BlockSpec(memory_space=pl.ANY)
    in_specs = _attn_prompt_specs(nq, seq, blk, with_pt)
    in_specs += [pl.BlockSpec((ATT_ROWS, 1), lambda n, pt: (0, 0)),
                 pl.BlockSpec((1, 1, QK_A), seq_index),
                 pl.BlockSpec((1, 1, QK_A), seq_index),
                 pl.BlockSpec((1, 1, W_A), seq_index),
                 any_spec, any_spec]
    grid_spec = pltpu.PrefetchScalarGridSpec(
        num_scalar_prefetch=1,
        grid=(nb * H_A * nq,),
        in_specs=in_specs,
        out_specs=[pl.BlockSpec((blk, DV_A), with_pt(lambda b, h, i: (b * nq + i, h))),
                   pl.BlockSpec((1, H_A, DV_A), seq_index)],
        scratch_shapes=[pltpu.VMEM((2, blk, 2 * DV_A), F32),
                        pltpu.VMEM((ATT_ROWS, n_pages * rows), F32),
                        pltpu.VMEM((2, n_pages, rows, DV_A), F32),
                        pltpu.VMEM((2, n_pages, rows, DV_A), F32),
                        pltpu.SemaphoreType.DMA((2, 2))],
    )
    return pl.pallas_call(
        functools.partial(_attn_fused_kernel, blk=blk, nq=nq, post_scale=post_scale, n_pages=n_pages,
                          layer=layer),
        grid_spec=grid_spec,
        out_shape=[jax.ShapeDtypeStruct((t, W_A), F32), jax.ShapeDtypeStruct((db, H_A, DV_A), F32)],
        name="attn_fused",
        compiler_params=pltpu.CompilerParams(
            dimension_semantics=("arbitrary",),
            vmem_limit_bytes=VMEM_LIMIT),
    )(pt_flat, par, slopes, nkeep, q, qf, k, v, kf, sub, slope_rows,
      dq.reshape(db, 1, QK_A), dkn.reshape(db, 1, QK_A), dvn.reshape(db, 1, W_A), ck, cv)


def _block_diag_rows(m):
    head = lax.broadcasted_iota(jnp.int32, m.shape, 1) // DK_C
    parts = [jnp.where(head == hp, m, 0.0).astype(BF16) for hp in range(H_C)]
    return jnp.concatenate(parts, axis=0)


def _mix_kernel(x_ref, oa_ref, u_ref, g_ref, kk_ref, vc_ref, qh_ref, sg_ref,
                ltri_ref, hh_ref, pw_ref, ps_ref, hg_ref, wo_ref,
                xo_ref, pool_ref, hst_ref,
                ubuf, kkbuf, gbuf, vcbuf, st_sc, inter_sc, xab_sc, *, tc):
    si = pl.program_id(1)
    ns = pl.num_programs(1)
    pad = HGRN_SUB
    nchunk = tc // HGRN_CHUNK

    @pl.when(si == 0)
    def _():
        st_sc[...] = jnp.zeros(st_sc.shape, F32)
        z = jnp.zeros((pad, W_B), F32)
        ubuf[0:pad, :] = z
        kkbuf[0:pad, :] = z
        gbuf[0:pad, :] = z
        vcbuf[0:pad, :] = z

    u = u_ref[...]
    ubuf[pad:pad + tc, :] = u
    acc = ubuf[...]
    wsum = {}
    for k, win in enumerate(POOL_WINDOWS):
        acc = acc + pltpu.roll(acc, 2 ** k, 0)
        wsum[win] = acc[pad:, :]
    pos = (si * tc + lax.broadcasted_iota(jnp.int32, (tc, 1), 0)).astype(F32)
    lane_b = lax.broadcasted_iota(jnp.int32, (tc, W_B), 1)
    pooled = None
    for gi, win in reversed(list(enumerate(POOL_WINDOWS))):
        term = wsum[win] * (1.0 / jnp.minimum(pos + 1.0, float(win)))
        pooled = term if pooled is None else jnp.where(lane_b < (gi + 1) * C_B, term, pooled)
    pooled = pooled - u
    ob = _dot(pooled.astype(BF16), pw_ref[...]) * ps_ref[...]
    ubuf[0:pad, :] = ubuf[tc:tc + pad, :]

    x_ab = (x_ref[...]
            + _dot((oa_ref[...] * sg_ref[:, 0:W_A]).astype(BF16), wo_ref[0:W_A, :])
            + _dot((ob * sg_ref[:, W_A:W_A + W_B]).astype(BF16), wo_ref[W_A:W_A + W_B, :]))
    xab_sc[...] = x_ab

    g = g_ref[...]
    kk = kk_ref[...]
    vc = vc_ref[...]
    qh = qh_ref[...]
    ltri = ltri_ref[...]
    ghi, gmid, glo = _split3(g)
    gc = _dot(ltri, ghi) + _dot(ltri, gmid) + _dot(ltri, glo)
    gc3 = gc.reshape(nchunk, HGRN_CHUNK, W_C)

    def chunk_row(idx):
        r = jnp.broadcast_to(gc3[:, idx:idx + 1, :], (nchunk, HGRN_CHUNK, W_C))
        return r.reshape(tc, W_C)

    rc = lax.broadcasted_iota(jnp.int32, (tc, 1), 0) % HGRN_CHUNK
    hh = hh_ref[...]
    t_idx = lax.broadcasted_iota(jnp.int32, (HGRN_CHUNK, H_C * HGRN_CHUNK), 0)
    s_idx = lax.broadcasted_iota(jnp.int32, (HGRN_CHUNK, H_C * HGRN_CHUNK), 1) % HGRN_CHUNK
    chunks = [slice(c * HGRN_CHUNK, (c + 1) * HGRN_CHUNK) for c in range(nchunk)]

    def hgrn_projection(o):
        ss = _dot((o * o).astype(BF16), hh)
        ocn = o * lax.rsqrt(ss * (1.0 / DV_C) + EPS) * hg_ref[...]
        return _dot((ocn * sg_ref[:, W_A + W_B:]).astype(BF16), wo_ref[W_A + W_B:, :])

    rmid = chunk_row(HGRN_CHUNK // 2 - 1)
    eg = jnp.exp(gc)
    qe = qh * eg
    kdec = kk * jnp.exp(chunk_row(HGRN_CHUNK - 1) - gc)
    qt = qh * jnp.exp(gc - rmid)
    kt = kk * jnp.exp(rmid - gc)
    bdmask = (lax.broadcasted_iota(jnp.int32, (W_C, W_C), 0) // DK_C
              == lax.broadcasted_iota(jnp.int32, (W_C, W_C), 1) // DK_C)
    outs = []
    for c, sl in enumerate(chunks):
        a = _dot_nt(qt[sl].astype(BF16), _block_diag_rows(kt[sl]))
        a = jnp.where(t_idx >= s_idx, a, 0.0)
        st = st_sc[...]
        inter = _dot_nt(qe[sl].astype(BF16), st.astype(BF16))
        inter_sc[sl, :] = inter
        outs.append(_dot(a.astype(BF16), _block_diag_rows(vc[sl])) + inter)
        last = c * HGRN_CHUNK + HGRN_CHUNK - 1
        upd = _dot_tn(vc[sl].astype(BF16), kdec[sl].astype(BF16))
        st_sc[...] = st * eg[last:last + 1, :] + jnp.where(bdmask, upd, 0.0)
    xo_ref[...] = x_ab + hgrn_projection(jnp.concatenate(outs, axis=0))

    safe = jnp.max(jnp.abs(gc - rmid)) <= HGRN_SAFE_EXP

    @pl.when(jnp.logical_not(safe))
    def _():
        upper = rc >= 32
        r31 = chunk_row(31)
        qt1 = jnp.where(upper, qh * jnp.exp(jnp.minimum(gc - r31, 0.0)), 0.0)
        kt1 = jnp.where(upper, 0.0, kk * jnp.exp(jnp.minimum(r31 - gc, 0.0)))
        ref2 = jnp.where(upper, chunk_row(47), chunk_row(15))
        odd = ((rc // HGRN_SUB) % 2) == 1
        qt2 = jnp.where(odd, qh * jnp.exp(jnp.minimum(gc - ref2, 0.0)), 0.0)
        kt2 = jnp.where(odd, 0.0, kk * jnp.exp(jnp.minimum(ref2 - gc, 0.0)))
        tb = t_idx // HGRN_SUB
        mask2 = ((tb % 2) == 1) & (s_idx // HGRN_SUB == tb - 1)

        kkbuf[pad:pad + tc, :] = kk
        gbuf[pad:pad + tc, :] = gc
        vcbuf[pad:pad + tc, :] = vc
        r16 = rc % HGRN_SUB
        od = jnp.zeros((tc, W_C), F32)
        for d in range(HGRN_SUB):
            kks = kkbuf[pad - d:pad - d + tc, :]
            gs = gbuf[pad - d:pad - d + tc, :]
            vcs = vcbuf[pad - d:pad - d + tc, :]
            xd = jnp.where(r16 >= d, qh * kks * jnp.exp(jnp.minimum(gc - gs, 0.0)), 0.0)
            od = od + _dot(xd.astype(BF16), hh) * vcs
        redo = []
        for sl in chunks:
            a1 = _dot_nt(qt1[sl].astype(BF16), _block_diag_rows(kt1[sl]))
            a2 = _dot_nt(qt2[sl].astype(BF16), _block_diag_rows(kt2[sl]))
            a_off = a1 + jnp.where(mask2, a2, 0.0)
            redo.append(_dot(a_off.astype(BF16), _block_diag_rows(vc[sl])) + od[sl] + inter_sc[sl, :])
        xo_ref[...] = xab_sc[...] + hgrn_projection(jnp.concatenate(redo, axis=0))

    @pl.when(si == ns - 1)
    def _():
        pool_ref[0] = u[tc - POOL_BUF:, :]
        hst_ref[0] = st_sc[...]


def _mix_prompt(x, oa, u, g, kk, vc, qh, sg, ltri, hh, pw, ps, hg, wo, layer, nb, seq, tc):
    t = nb * seq
    ns = seq // tc
    row = lambda b, s: (b * ns + s, 0)
    fix = lambda b, s: (0, 0)
    lay = lambda b, s: (layer, 0, 0)
    in_specs = [pl.BlockSpec((tc, D_MODEL), row), pl.BlockSpec((tc, W_A), row)]
    in_specs += [pl.BlockSpec((tc, W_C), row) for _ in range(5)]
    in_specs += [
        pl.BlockSpec((tc, D_MODEL), row),
        pl.BlockSpec((tc, tc), fix),
        pl.BlockSpec((W_C, W_C), fix),
        pl.BlockSpec((None, W_B, W_B), lay),
        pl.BlockSpec((1, W_B), fix),
        pl.BlockSpec((1, W_C), fix),
        pl.BlockSpec((None, D_MODEL, D_MODEL), lay),
    ]
    return pl.pallas_call(
        functools.partial(_mix_kernel, tc=tc),
        grid=(nb, ns),
        in_specs=in_specs,
        out_specs=[
            pl.BlockSpec((tc, D_MODEL), row),
            pl.BlockSpec((1, POOL_BUF, W_B), lambda b, s: (b, 0, 0)),
            pl.BlockSpec((1, W_C, W_C), lambda b, s: (b, 0, 0)),
        ],
        out_shape=[
            jax.ShapeDtypeStruct((t, D_MODEL), F32),
            jax.ShapeDtypeStruct((nb, POOL_BUF, W_B), F32),
            jax.ShapeDtypeStruct((nb, W_C, W_C), F32),
        ],
        scratch_shapes=[
            pltpu.VMEM((tc + HGRN_SUB, W_B), F32),
            pltpu.VMEM((tc + HGRN_SUB, W_C), F32),
            pltpu.VMEM((tc + HGRN_SUB, W_C), F32),
            pltpu.VMEM((tc + HGRN_SUB, W_C), F32),
            pltpu.VMEM((W_C, W_C), F32),
            pltpu.VMEM((tc, W_C), F32),
            pltpu.VMEM((tc, D_MODEL), F32),
        ],
        name="mix_prompt",
        compiler_params=pltpu.CompilerParams(
            dimension_semantics=("parallel", "arbitrary"), vmem_limit_bytes=VMEM_LIMIT),
    )(x, oa, u, g, kk, vc, qh, sg, ltri, hh, pw, ps, hg, wo)


ATT_ROWS = 2 * H_A


def _paged_attn_begin(slope, q_row, kn_row, vn_row, k_refs, s_sc, n_pages):
    rows = PAGE_SIZE * H_A
    past = n_pages * PAGE_SIZE

    r = lax.broadcasted_iota(jnp.int32, (ATT_ROWS, DV_A), 0)
    lane = lax.broadcasted_iota(jnp.int32, (ATT_ROWS, DV_A), 1)

    def head_rows(row):
        out = jnp.zeros((ATT_ROWS, DV_A), F32)
        for h in range(H_A):
            piece = jnp.broadcast_to(row[:, h * DV_A:(h + 1) * DV_A], (ATT_ROWS, DV_A))
            out = jnp.where(r % H_A == h, piece, out)
        return out

    q = jnp.where(lane // DK_A == r // (ATT_ROWS // 2), head_rows(q_row), 0.0)
    kn = head_rows(kn_row)
    vn = head_rows(vn_row)
    qb = q.astype(BF16)

    col = lax.broadcasted_iota(jnp.int32, (ATT_ROWS, rows), 1)
    rr = lax.broadcasted_iota(jnp.int32, (ATT_ROWS, rows), 0)
    valid = (col % H_A) == (rr % H_A)
    tok = col // H_A
    for j in range(n_pages):
        s = _dot_nt(qb, k_refs[j][...].astype(BF16))
        dist = (past - j * PAGE_SIZE - tok).astype(F32)
        s_sc[:, j * rows:(j + 1) * rows] = jnp.where(valid, s - slope * dist, NEG)
    s_self = jnp.sum(q * kn, axis=-1, keepdims=True)

    def finish(lam, sub, v_refs, post_scale):
        s_all = s_sc[...]
        m = jnp.maximum(jnp.max(s_all, axis=-1, keepdims=True), s_self)
        p_self = jnp.exp(s_self - m)
        l = p_self
        acc = p_self * vn
        for j in range(n_pages):
            p = jnp.exp(s_all[:, j * rows:(j + 1) * rows] - m)
            l = l + jnp.sum(p, axis=-1, keepdims=True)
            acc = acc + _dot(p.astype(BF16), v_refs[j][...].astype(BF16))
        o_maps = acc / l
        half = ATT_ROWS // 2
        o = o_maps[0:half] - lam * o_maps[half:ATT_ROWS]
        on = o * lax.rsqrt(jnp.mean(o * o, axis=-1, keepdims=True) + EPS) * sub
        return (on * post_scale)[0:H_A]

    return finish


def _paged_attn_kernel(pt_ref, lam_ref, slope_ref, q_ref, kn_ref, vn_ref, sub_ref, *rest,
                       n_pages, post_scale):
    del pt_ref
    k_refs = rest[:n_pages]
    v_refs = rest[n_pages:2 * n_pages]
    o_ref = rest[2 * n_pages]
    s_sc = rest[2 * n_pages + 1]
    finish = _paged_attn_begin(slope_ref[...], q_ref[0], kn_ref[0], vn_ref[0], k_refs, s_sc, n_pages)
    o_ref[0] = finish(lam_ref[0], sub_ref[...], v_refs, post_scale)


def _attn_paged(pt_flat, lam, slope_rows, q, kn, vn, sub, ck, cv, layer, n_pages, post_scale):
    nb = q.shape[0]
    rows = PAGE_SIZE * H_A
    smem = pl.BlockSpec(memory_space=pltpu.SMEM)
    row3 = lambda b, pt: (b, 0, 0)

    def page_spec(j):
        return pl.BlockSpec((None, None, rows, DV_A),
                            lambda b, pt, j=j: (layer, pt[b * n_pages + j], 0, 0))

    in_specs = [smem,
                pl.BlockSpec((ATT_ROWS, 1), lambda b, pt: (0, 0)),
                pl.BlockSpec((1, 1, QK_A), row3),
                pl.BlockSpec((1, 1, QK_A), row3),
                pl.BlockSpec((1, 1, W_A), row3),
                pl.BlockSpec((1, DV_A), lambda b, pt: (0, 0))]
    in_specs += [page_spec(j) for j in range(n_pages)] * 2
    grid_spec = pltpu.PrefetchScalarGridSpec(
        num_scalar_prefetch=1,
        grid=(nb,),
        in_specs=in_specs,
        out_specs=pl.BlockSpec((1, H_A, DV_A), row3),
        scratch_shapes=[pltpu.VMEM((ATT_ROWS, n_pages * rows), F32)],
    )
    return pl.pallas_call(
        functools.partial(_paged_attn_kernel, n_pages=n_pages, post_scale=post_scale),
        grid_spec=grid_spec,
        out_shape=jax.ShapeDtypeStruct((nb, H_A, DV_A), F32),
        name="attn_paged",
        compiler_params=pltpu.CompilerParams(
            dimension_semantics=("arbitrary",), vmem_limit_bytes=VMEM_LIMIT),
    )(pt_flat, lam, slope_rows, q.reshape(nb, 1, QK_A), kn.reshape(nb, 1, QK_A),
      vn.reshape(nb, 1, W_A), sub, *([ck] * n_pages), *([cv] * n_pages))


def _dec_mix_kernel(x_ref, oa_ref, u_ref, gt_ref, kkt_ref, vct_ref, qht_ref, sg_ref, sp_ref, sh_ref,
                    pall_ref, hall_ref, pw_ref, ps_ref, hgc_ref, wo_ref,
                    xo_ref, po_ref, ho_ref, ot_sc, *, past):
    del pall_ref, hall_ref
    h = pl.program_id(0)
    nb = x_ref.shape[0]

    @pl.when(h == 0)
    def _():
        u = u_ref[...]
        lane_b = lax.broadcasted_iota(jnp.int32, (nb, W_B), 1)
        acc = u
        wsum = {}
        for j in range(1, max(POOL_WINDOWS)):
            acc = acc + sp_ref[POOL_BUF - j]
            if j + 1 in POOL_WINDOWS:
                wsum[j + 1] = acc
        pooled = None
        for gi, win in reversed(list(enumerate(POOL_WINDOWS))):
            term = wsum[win] * (1.0 / min(past + 1, win))
            pooled = term if pooled is None else jnp.where(lane_b < (gi + 1) * C_B, term, pooled)
        pooled = pooled - u
        for j in range(POOL_BUF - 1):
            po_ref[j] = sp_ref[j + 1]
        po_ref[POOL_BUF - 1] = u
        ob = _dot(pooled.astype(BF16), pw_ref[...]) * ps_ref[...]
        xo_ref[...] = (x_ref[...]
                       + _dot((oa_ref[...] * sg_ref[:, 0:W_A]).astype(BF16), wo_ref[0:W_A, :])
                       + _dot((ob * sg_ref[:, W_A:W_A + W_B]).astype(BF16), wo_ref[W_A:W_A + W_B, :]))

    base = pl.multiple_of(h * DK_C, DK_C)
    vt = vct_ref[pl.ds(base, DV_C), :]

    def per_d(d, acc):
        r = base + d
        s_new = jnp.exp(gt_ref[pl.ds(r, 1), :]) * sh_ref[d] + kkt_ref[pl.ds(r, 1), :] * vt
        ho_ref[d] = s_new
        return acc + qht_ref[pl.ds(r, 1), :] * s_new

    ot_sc[pl.ds(base, DV_C), :] = lax.fori_loop(0, DK_C, per_d, jnp.zeros((DV_C, nb), F32))

    @pl.when(h == pl.num_programs(0) - 1)
    def _():
        ot = ot_sc[...].reshape(H_C, DV_C, nb)
        otn = ot * lax.rsqrt(jnp.mean(ot * ot, axis=1, keepdims=True) + EPS)
        mix_t = otn.reshape(W_C, nb) * hgc_ref[...] * sg_ref[:, W_A + W_B:].T
        xo_ref[...] += _dot_tn(mix_t.astype(BF16), wo_ref[W_A + W_B:, :])


def _mix_decode(x, oa, u, gt, kkt, vct, qht, sg, sp, sh, pool_all, hgrn_all, pw, ps, hgc, wo,
                layer, past):
    nb = x.shape[0]
    fix = lambda h: (0, 0)
    pool_spec = pl.BlockSpec((None, POOL_BUF, nb, W_B), lambda h: (layer, 0, 0, 0))
    state_spec = pl.BlockSpec((None, None, DK_C, DV_C, nb), lambda h: (layer, h, 0, 0, 0))
    any_spec = pl.BlockSpec(memory_space=pl.ANY)
    in_specs = [
        pl.BlockSpec((nb, D_MODEL), fix),
        pl.BlockSpec((nb, W_A), fix),
        pl.BlockSpec((nb, W_B), fix),
        pl.BlockSpec((W_C, nb), fix),
        pl.BlockSpec((W_C, nb), fix),
        pl.BlockSpec((W_C, nb), fix),
        pl.BlockSpec((W_C, nb), fix),
        pl.BlockSpec((nb, D_MODEL), fix),
        pool_spec,
        state_spec,
        any_spec,
        any_spec,
        pl.BlockSpec((None, W_B, W_B), lambda h: (layer, 0, 0)),
        pl.BlockSpec((1, W_B), fix),
        pl.BlockSpec((None, W_C, 1), lambda h: (layer, 0, 0)),
        pl.BlockSpec((None, D_MODEL, D_MODEL), lambda h: (layer, 0, 0)),
    ]
    return pl.pallas_call(
        functools.partial(_dec_mix_kernel, past=past),
        grid=(H_C,),
        in_specs=in_specs,
        out_specs=[pl.BlockSpec((nb, D_MODEL), fix), pool_spec, state_spec],
        out_shape=[
            jax.ShapeDtypeStruct((nb, D_MODEL), F32),
            jax.ShapeDtypeStruct(pool_all.shape, F32),
            jax.ShapeDtypeStruct(hgrn_all.shape, F32),
        ],
        scratch_shapes=[pltpu.VMEM((W_C, nb), F32)],
        input_output_aliases={10: 1, 11: 2},
        name="mix_decode",
        compiler_params=pltpu.CompilerParams(
            dimension_semantics=("arbitrary",), vmem_limit_bytes=VMEM_LIMIT),
    )(x, oa, u, gt, kkt, vct, qht, sg, sp, sh, pool_all, hgrn_all, pw, ps, hgc, wo)


def _block_ones(n, blk):
    idx = np.arange(n) // blk
    return jnp.asarray((idx[:, None] == idx[None, :]).astype(np.float32), dtype=BF16)


def _chunk_lower_tri(n, blk):
    idx = np.arange(n)
    same = (idx[:, None] // blk) == (idx[None, :] // blk)
    return jnp.asarray((same & (idx[:, None] >= idx[None, :])).astype(np.float32), dtype=BF16)


def kernel(x_prompt, x_sample, cache_k, cache_v, state_pool, state_hgrn, page_table,
           ln_gain, w_in, qn_gain, kn_gain, lam_q1, lam_k1, lam_q2, lam_k2, subln_gain,
           pool_w, pool_scale, hgrn_lb, hgrn_norm_gain, w_out):
    nb, seq, _ = x_prompt.shape
    t = nb * seq
    db = x_sample.shape[0]
    n_pages = page_table.shape[1]
    past = n_pages * PAGE_SIZE
    n_phys = cache_k.shape[1]

    tm = min(512, t)
    blk = min(512, seq)
    tc = min(256, seq)

    w_in_b = w_in.astype(BF16)
    w_out_b = w_out.astype(BF16)
    sm = jax.nn.softmax(hgrn_lb.astype(F32), axis=0)
    lb_all = jnp.cumsum(sm, axis=0) - sm[0]
    lam_init = [0.8 - 0.6 * math.exp(-0.3 * l) for l in range(DEPTH)]
    lam_all = (jnp.exp(jnp.sum(lam_q1.astype(F32) * lam_k1.astype(F32), axis=-1))
               - jnp.exp(jnp.sum(lam_q2.astype(F32) * lam_k2.astype(F32), axis=-1))
               + jnp.asarray(lam_init, F32))
    slopes_np = np.asarray([2.0 ** (-8.0 * (h + 1) / H_A) for h in range(H_A)], np.float32)
    slopes = jnp.asarray(slopes_np)
    slope_rows = jnp.asarray(slopes_np[np.arange(ATT_ROWS) % H_A].reshape(ATT_ROWS, 1))
    score_bound = (1.0125 * DK_A ** 0.5) * jnp.max(
        jnp.abs(qn_gain.astype(F32) * kn_gain.astype(F32)), axis=-1)
    sb_hi = score_bound.astype(BF16).astype(F32)
    sb_mid = (score_bound - sb_hi).astype(BF16).astype(F32)
    sb_lo = (score_bound - sb_hi - sb_mid).astype(BF16).astype(F32)
    attn_par = jnp.stack([lam_all, sb_hi, sb_mid, sb_lo], axis=1)
    kfeat = _attn_k_features(blk)
    qfeat = [_attn_q_features(blk, slopes_np, attn_par[l, 1:]) for l in range(DEPTH)]
    nkeep = _attn_blocks_kept(blk, seq // blk, slopes_np)
    fuse_decode = nb * H_A * (seq // blk) == db
    qg = jnp.tile(qn_gain.astype(F32), (1, QK_A // DK_A))
    kg = jnp.tile(kn_gain.astype(F32), (1, QK_A // DK_A))
    hg_tiled = jnp.tile(hgrn_norm_gain.astype(F32), (1, H_C))
    eye_g = jnp.eye(G_B, dtype=F32)
    pw_bd = jnp.einsum('lgcd,gh->lgchd', pool_w.astype(F32), eye_g).reshape(DEPTH, W_B, W_B).astype(BF16)
    hh = _block_ones(W_C, DV_C)
    ltri = _chunk_lower_tri(tc, HGRN_CHUNK)
    ck = cache_k.reshape(DEPTH, n_phys, PAGE_SIZE * H_A, 2 * DK_A)
    cv = cache_v.reshape(DEPTH, n_phys, PAGE_SIZE * H_A, DV_A)
    pt_flat = page_table.reshape(-1).astype(jnp.int32)

    xp = x_prompt.reshape(t, D_MODEL)
    xs = x_sample.reshape(db, D_MODEL)
    k_all = jnp.zeros((DEPTH, t * H_A, 2 * DK_A), F32)
    v_all = jnp.zeros((DEPTH, t * H_A, DV_A), F32)
    pools, states = [], []
    dks, dvs = [], []
    sp_in = jnp.transpose(state_pool, (0, 2, 1, 3))
    sh_in = jnp.transpose(state_hgrn, (0, 2, 3, 4, 1))
    dpool_all = jnp.zeros(sp_in.shape, F32)
    dstate_all = jnp.zeros(sh_in.shape, F32)
    hg_col = jnp.tile(hgrn_norm_gain.astype(F32), (1, H_C)).reshape(DEPTH, W_C, 1)
    for l in range(DEPTH):
        lng = ln_gain[l].reshape(1, D_MODEL).astype(F32)
        lb = lb_all[l].reshape(1, W_C)
        sub = subln_gain[l].reshape(1, DV_A).astype(F32)
        ps = pool_scale[l].reshape(1, W_B).astype(F32)
        post = 1.0 - lam_init[l]

        q, k_all, kb, v_all, vb, u, g, kk, vc, qh, sg = _inproj_prompt(
            xp, lng, w_in_b, qg[l:l + 1], kg[l:l + 1], lb, hh, k_all, v_all, l, tm)
        dq, dk, dv, du, dgt, dkkt, dvct, dqht, dsg = _inproj_decode(
            xs, lng, w_in_b, qg[l:l + 1], kg[l:l + 1], lb, hh, l)

        def attend(par, qf, q, kb, vb, sub, dq, dk, dv, shifted, layer=l, post=post):
            if shifted and fuse_decode:
                return _attn_fused(pt_flat, par, slopes, nkeep, q, qf, kb, vb, kfeat, sub, slope_rows,
                                   dq, dk, dv, ck, cv, layer, n_pages, nb, seq, blk, post)
            oa = _attn_prompt(par, slopes, nkeep, q, qf, kb, vb, kfeat, sub, nb, seq, blk, post, shifted)
            doa = _attn_paged(pt_flat, par[0:1], slope_rows, dq, dk, dv, sub, ck, cv, layer,
                              n_pages, post)
            return oa, doa

        oa, doa = lax.cond(
            score_bound[l] <= ATTN_SAFE_SHIFT,
            functools.partial(attend, shifted=True), functools.partial(attend, shifted=False),
            attn_par[l], qfeat[l], q, kb, vb, sub, dq, dk, dv)
        xp, pool_new, st = _mix_prompt(xp, oa, u, g, kk, vc, qh, sg, ltri, hh, pw_bd, ps,
                                       hg_tiled[l:l + 1], w_out_b, l, nb, seq, tc)
        pools.append(pool_new)
        states.append(st)
        xs, dpool_all, dstate_all = _mix_decode(
            xs, doa.reshape(db, W_A), du, dgt, dkkt, dvct, dqht, dsg, sp_in, sh_in,
            dpool_all, dstate_all, pw_bd, ps, hg_col, w_out_b, l, past)
        dks.append(dk)
        dvs.append(dv)

    y_prompt = xp.reshape(nb, seq, D_MODEL)
    y_sample = xs.reshape(db, 1, D_MODEL)
    k_prompt = k_all.reshape(DEPTH, nb, seq, H_A, 2 * DK_A)
    v_prompt = v_all.reshape(DEPTH, nb, seq, H_A, DV_A)
    k_sample = jnp.stack(dks).reshape(DEPTH, db, 1, H_A, 2 * DK_A)
    v_sample = jnp.stack(dvs).reshape(DEPTH, db, 1, H_A, DV_A)
    pool_prompt = jnp.stack(pools)
    pool_sample = jnp.transpose(dpool_all, (0, 2, 1, 3))
    st_all = jnp.stack(states).reshape(DEPTH, nb, H_C, DV_C, H_C, DK_C)
    hgrn_prompt = jnp.stack([st_all[:, :, h, :, h, :] for h in range(H_C)], axis=2).swapaxes(-1, -2)
    hgrn_sample = jnp.transpose(dstate_all, (0, 4, 1, 2, 3))
    return (y_prompt, y_sample, k_prompt, v_prompt, k_sample, v_sample,
            pool_prompt, pool_sample, hgrn_prompt, hgrn_sample)
```

```python
import functools
import math

import numpy as np
import jax
import jax.numpy as jnp
from jax import lax
from jax.experimental import pallas as pl
from jax.experimental.pallas import tpu as pltpu

F32 = jnp.float32
BF16 = jnp.bfloat16

D_MODEL = 1024
DEPTH = 4
PAGE_SIZE = 128
H_A = 4
DK_A = 64
DV_A = 128
W_A = H_A * DV_A
QK_A = H_A * 2 * DK_A
G_B = 4
POOL_WINDOWS = (2, 4, 8, 16)
W_B = 256
C_B = W_B // G_B
POOL_BUF = max(POOL_WINDOWS) - 1
H_C = 4
W_C = 256
DK_C = 64
DV_C = 64
D_IN = 2 * QK_A + W_A + W_B + 3 * W_C + D_MODEL
HGRN_CHUNK = 64
HGRN_SUB = 16
HGRN_SAFE_EXP = 60.0
ATTN_SAFE_SHIFT = 20.0
EPS = 1e-6
NEG = -1e30
assert POOL_WINDOWS == tuple(2 ** (k + 1) for k in range(len(POOL_WINDOWS)))

C_Q, C_K, C_V, C_U = 0, QK_A, 2 * QK_A, 2 * QK_A + W_A
C_F = C_U + W_B
C_I = C_F + W_C
C_QC = C_I + W_C
C_GATE = C_QC + W_C

V7X_VMEM_BYTES = 64 * 1024 * 1024
VMEM_LIMIT = 56 * 1024 * 1024


def _dot(a, b):
    return jnp.dot(a, b, preferred_element_type=F32)


def _dot_nt(a, b):
    return lax.dot_general(a, b, (((1,), (1,)), ((), ())), preferred_element_type=F32)


def _dot_tn(a, b):
    return lax.dot_general(a, b, (((0,), (0,)), ((), ())), preferred_element_type=F32)


def _split3(x):
    hi = x.astype(BF16)
    r = x - hi.astype(F32)
    mid = r.astype(BF16)
    lo = (r - mid.astype(F32)).astype(BF16)
    return hi, mid, lo


def _sigmoid(x):
    return 1.0 / (1.0 + jnp.exp(-x))


def _inproj_body(x_ref, lng_ref, w_ref, qg_ref, kg_ref, lb_ref, bd_ref, channel_major_vc=False):
    x = x_ref[...]
    h = x * lax.rsqrt(jnp.mean(x * x, axis=-1, keepdims=True) + EPS) * lng_ref[...]
    hb = h.astype(BF16)
    bd = bd_ref[...]

    def proj(lo, hi):
        return _dot(hb, w_ref[:, lo:hi])

    def head_norm(z, gain):
        sq = (z * z).astype(BF16)
        bw = bd.shape[0]
        ss = jnp.concatenate([_dot(sq[:, j:j + bw], bd) for j in range(0, z.shape[1], bw)], axis=1)
        return z * lax.rsqrt(ss * (1.0 / DK_A) + EPS) * gain

    qn = head_norm(proj(C_Q, C_K), qg_ref[...]) * (DK_A ** -0.5)
    kn = head_norm(proj(C_K, C_V), kg_ref[...])
    v = proj(C_V, C_U)
    u = proj(C_U, C_F)
    lb = lb_ref[...]
    sig = _sigmoid(proj(C_F, C_I))
    g = jnp.log(lb + (1.0 - lb) * sig)
    kk = (1.0 - lb) * (1.0 - sig)
    if channel_major_vc:
        vc = lax.dot_general(w_ref[:, C_I:C_QC], hb, (((0,), (1,)), ((), ())),
                             preferred_element_type=F32)
    else:
        vc = proj(C_I, C_QC)
    qc = proj(C_QC, C_GATE)
    qh = qc * _sigmoid(qc)
    gate = proj(C_GATE, D_IN)
    sg = gate * _sigmoid(gate)
    return qn, kn, v, u, g, kk, vc, qh, sg


def _inproj_prompt_kernel(x_ref, lng_ref, w_ref, qg_ref, kg_ref, lb_ref, bd_ref, kall_ref, vall_ref,
                          q_ref, k_ref, kb_ref, v_ref, vb_ref, u_ref, g_ref, kk_ref, vc_ref,
                          qh_ref, sg_ref):
    del kall_ref, vall_ref
    qn, kn, v, u, g, kk, vc, qh, sg = _inproj_body(x_ref, lng_ref, w_ref, qg_ref, kg_ref, lb_ref, bd_ref)
    tm = x_ref.shape[0]
    q_ref[...] = qn.astype(BF16)
    for h in range(H_A):
        k_ref[pl.ds(h, tm, stride=H_A), :] = kn[:, h * DV_A:(h + 1) * DV_A]
        v_ref[pl.ds(h, tm, stride=H_A), :] = v[:, h * DV_A:(h + 1) * DV_A]
    kb_ref[...] = kn.astype(BF16)
    vb_ref[...] = v.astype(BF16)
    u_ref[...] = u
    g_ref[...] = g
    kk_ref[...] = kk
    vc_ref[...] = vc
    qh_ref[...] = qh
    sg_ref[...] = sg


def _inproj_decode_kernel(x_ref, lng_ref, w_ref, qg_ref, kg_ref, lb_ref, bd_ref,
                          q_ref, k_ref, v_ref, u_ref, gt_ref, kkt_ref, vct_ref, qht_ref, sg_ref):
    qn, kn, v, u, g, kk, vct, qh, sg = _inproj_body(x_ref, lng_ref, w_ref, qg_ref, kg_ref, lb_ref, bd_ref,
                                                    channel_major_vc=True)
    q_ref[...] = qn
    k_ref[...] = kn
    v_ref[...] = v
    u_ref[...] = u
    gt_ref[...] = g.T
    kkt_ref[...] = kk.T
    vct_ref[...] = vct
    qht_ref[...] = qh.T
    sg_ref[...] = sg


def _inproj_in_specs(tm, layer):
    row = lambda i: (i, 0)
    fix = lambda i: (0, 0)
    return [
        pl.BlockSpec((tm, D_MODEL), row),
        pl.BlockSpec((1, D_MODEL), fix),
        pl.BlockSpec((None, D_MODEL, D_IN), lambda i: (layer, 0, 0)),
        pl.BlockSpec((1, QK_A), fix),
        pl.BlockSpec((1, QK_A), fix),
        pl.BlockSpec((1, W_C), fix),
        pl.BlockSpec((W_C, W_C), fix),
    ]


def _inproj_prompt(x, lng, w, qg, kg, lb, bd, k_all, v_all, layer, tm):
    t = x.shape[0]
    row = lambda i: (i, 0)

    def plain(wd, dt):
        return pl.BlockSpec((tm, wd), row), jax.ShapeDtypeStruct((t, wd), dt)

    def stacked(arr):
        return (pl.BlockSpec((None, tm * H_A, DV_A), lambda i: (layer, i, 0)),
                jax.ShapeDtypeStruct(arr.shape, arr.dtype))

    outs = [plain(QK_A, BF16), stacked(k_all), plain(QK_A, BF16), stacked(v_all), plain(W_A, BF16),
            plain(W_B, F32), plain(W_C, F32), plain(W_C, F32), plain(W_C, F32), plain(W_C, F32),
            plain(D_MODEL, F32)]
    n_in = len(_inproj_in_specs(tm, layer))
    any_spec = pl.BlockSpec(memory_space=pl.ANY)
    return pl.pallas_call(
        _inproj_prompt_kernel,
        grid=(t // tm,),
        in_specs=_inproj_in_specs(tm, layer) + [any_spec, any_spec],
        out_specs=[o[0] for o in outs],
        out_shape=[o[1] for o in outs],
        input_output_aliases={n_in: 1, n_in + 1: 3},
        name="inproj_prompt",
        compiler_params=pltpu.CompilerParams(
            dimension_semantics=("parallel",), vmem_limit_bytes=VMEM_LIMIT),
    )(x, lng, w, qg, kg, lb, bd, k_all, v_all)


def _inproj_decode(x, lng, w, qg, kg, lb, bd, layer):
    t = x.shape[0]
    row = lambda i: (i, 0)
    shapes = [(t, QK_A), (t, QK_A), (t, W_A), (t, W_B), (W_C, t), (W_C, t), (W_C, t), (W_C, t),
              (t, D_MODEL)]
    return pl.pallas_call(
        _inproj_decode_kernel,
        grid=(1,),
        in_specs=_inproj_in_specs(t, layer),
        out_specs=[pl.BlockSpec(s, row) for s in shapes],
        out_shape=[jax.ShapeDtypeStruct(s, F32) for s in shapes],
        name="inproj_decode",
        compiler_params=pltpu.CompilerParams(
            dimension_semantics=("arbitrary",), vmem_limit_bytes=VMEM_LIMIT),
    )(x, lng, w, qg, kg, lb, bd)


def _attn_kernel(par_ref, slope_ref, nkeep_ref, q_ref, qf_ref, k_ref, v_ref, kf_ref, sub_ref, o_ref,
                 m_sc, l_sc, acc_sc, *, blk, post_scale):
    del nkeep_ref, qf_ref, kf_ref
    hd = pl.program_id(1)
    qi = pl.program_id(2)
    slope = slope_ref[hd]
    lam = par_ref[0]
    q = q_ref[...]
    lane = lax.broadcasted_iota(jnp.int32, q.shape, 1)
    zero = jnp.zeros_like(q)
    qm = (jnp.where(lane < DK_A, q, zero), jnp.where(lane >= DK_A, q, zero))
    rel = (lax.broadcasted_iota(jnp.int32, (blk, blk), 0)
           - lax.broadcasted_iota(jnp.int32, (blk, blk), 1))
    srel = slope * rel.astype(F32)
    m_sc[...] = jnp.full(m_sc.shape, NEG, F32)
    l_sc[...] = jnp.zeros(l_sc.shape, F32)
    acc_sc[...] = jnp.zeros(acc_sc.shape, F32)

    def step(kj, masked):
        start = pl.multiple_of(kj * blk, blk)
        kb = k_ref[pl.ds(start, blk), :]
        vb = v_ref[pl.ds(start, blk), :]
        off = slope * ((qi - kj) * blk).astype(F32)
        for c in range(2):
            s = _dot_nt(qm[c], kb) - srel - off
            if masked:
                s = jnp.where(rel >= 0, s, NEG)
            m_old = m_sc[c]
            m_new = jnp.maximum(m_old, jnp.max(s, axis=-1, keepdims=True))
            alpha = jnp.exp(m_old - m_new)
            p = jnp.exp(s - m_new)
            l_sc[c] = alpha * l_sc[c] + jnp.sum(p, axis=-1, keepdims=True)
            acc_sc[c] = alpha * acc_sc[c] + _dot(p.astype(BF16), vb)
            m_sc[c] = m_new

    def body(kj, carry):
        step(kj, False)
        return carry

    lax.fori_loop(0, qi, body, 0)
    step(qi, True)
    o = acc_sc[0] / l_sc[0] - lam * (acc_sc[1] / l_sc[1])
    on = o * lax.rsqrt(jnp.mean(o * o, axis=-1, keepdims=True) + EPS) * sub_ref[...]
    o_ref[...] = on * post_scale


N_SHIFT_PARTS = 3
FEAT_SUB = 64


def _attn_k_features(blk):
    j = np.arange(blk)
    f = np.zeros((blk, DV_A), np.float32)
    f[:, 0] = 1.0
    f[:, 1] = 1.0
    f[:, 2] = j // FEAT_SUB
    f[:, 3] = j % FEAT_SUB
    f[:, 4:4 + N_SHIFT_PARTS] = 1.0
    return jnp.asarray(f, dtype=BF16)


ATTN_ZERO_EXP = -104.0


def _attn_blocks_kept(blk, nq, slopes_np):
    kept = []
    for slope in slopes_np:
        n = 1
        while n < nq and slope * ((n - 1) * blk + 1) <= -ATTN_ZERO_EXP:
            n += 1
        kept.append(n)
    return jnp.asarray(kept, jnp.int32)


def _attn_q_features(blk, slopes_np, shift_parts):
    i = np.arange(blk)
    f = np.zeros((H_A, blk, DV_A), np.float32)
    for h in range(H_A):
        f[h, :, 0] = -slopes_np[h] * FEAT_SUB * (i // FEAT_SUB)
        f[h, :, 1] = -slopes_np[h] * (i % FEAT_SUB)
        f[h, :, 2] = slopes_np[h] * FEAT_SUB
        f[h, :, 3] = slopes_np[h]
    lane = np.arange(DV_A)
    out = jnp.asarray(f)
    for part in range(N_SHIFT_PARTS):
        out = jnp.where(lane == 4 + part, -shift_parts[part], out)
    return out.astype(BF16)


def _attn_shift_body(hd, qi, par_ref, slope_ref, nkeep_ref, q_ref, qf_ref, k_ref, v_ref, kf_ref, sub_ref,
                     o_ref, acc_sc, blk, post_scale, between_diagonal=None):
    slope = slope_ref[hd]
    lam = par_ref[0]
    q = q_ref[...]
    lane = lax.broadcasted_iota(jnp.int32, q.shape, 1)
    qfb = qf_ref[...]
    zero = jnp.zeros_like(q)
    qe = (jnp.concatenate([jnp.where(lane < DK_A, q, zero), qfb], axis=1),
          jnp.concatenate([jnp.where(lane >= DK_A, q, zero), qfb], axis=1))
    kf = kf_ref[...]
    ones = jnp.ones((blk, DV_A), BF16)

    def scores(kj):
        start = pl.multiple_of(kj * blk, blk)
        kb = jnp.concatenate([k_ref[pl.ds(start, blk), :], kf], axis=1)
        return [_dot_nt(qe[c], kb) for c in range(2)]

    def accumulate(kj, s):
        start = pl.multiple_of(kj * blk, blk)
        vb = jnp.concatenate([v_ref[pl.ds(start, blk), :], ones], axis=1)
        cstep = slope * ((qi - kj) * blk).astype(F32)
        for c in range(2):
            acc_sc[c] += _dot(jnp.exp(s[c] - cstep).astype(BF16), vb)

    half = blk // 2
    diag_start = pl.multiple_of(qi * blk, blk)
    kb_diag = jnp.concatenate([k_ref[pl.ds(diag_start, blk), :], kf], axis=1)
    s_top = [_dot_nt(qe[c][0:half], kb_diag[0:half]) for c in range(2)]
    s_bot = [_dot_nt(qe[c][half:blk], kb_diag) for c in range(2)]
    if between_diagonal is not None:
        between_diagonal()
    vb_diag = jnp.concatenate([v_ref[pl.ds(diag_start, blk), :], ones], axis=1)
    rel_top = (lax.broadcasted_iota(jnp.int32, (half, half), 0)
               - lax.broadcasted_iota(jnp.int32, (half, half), 1))
    rel_bot = (lax.broadcasted_iota(jnp.int32, (half, blk), 0) + half
               - lax.broadcasted_iota(jnp.int32, (half, blk), 1))
    for c in range(2):
        p_top = jnp.where(rel_top >= 0, jnp.exp(s_top[c]), 0.0).astype(BF16)
        p_bot = jnp.where(rel_bot >= 0, jnp.exp(s_bot[c]), 0.0).astype(BF16)
        acc_sc[c, 0:half, :] = _dot(p_top, vb_diag[0:half])
        acc_sc[c, half:blk, :] = _dot(p_bot, vb_diag)

    first = jnp.maximum(qi + 1 - nkeep_ref[hd], 0)
    n_off = qi - first

    @pl.when(n_off % 2 == 1)
    def _():
        accumulate(qi - 1, scores(qi - 1))

    def pair(i, carry):
        ka = first + 2 * i
        s_a = scores(ka)
        s_b = scores(ka + 1)
        accumulate(ka, s_a)
        accumulate(ka + 1, s_b)
        return carry

    lax.fori_loop(0, n_off // 2, pair, 0)
    a0 = acc_sc[0]
    a1 = acc_sc[1]
    o = a0[:, 0:DV_A] / a0[:, DV_A:] - lam * (a1[:, 0:DV_A] / a1[:, DV_A:])
    on = o * lax.rsqrt(jnp.mean(o * o, axis=-1, keepdims=True) + EPS) * sub_ref[...]
    o_ref[...] = on * post_scale


def _attn_shift_kernel(par_ref, slope_ref, nkeep_ref, q_ref, qf_ref, k_ref, v_ref, kf_ref, sub_ref, o_ref, acc_sc,
                       *, blk, post_scale):
    _attn_shift_body(pl.program_id(1), pl.program_id(2), par_ref, slope_ref, nkeep_ref, q_ref, qf_ref,
                     k_ref, v_ref, kf_ref, sub_ref, o_ref, acc_sc, blk, post_scale)


def _attn_fused_kernel(pt_ref, par_ref, slope_ref, nkeep_ref, q_ref, qf_ref, k_ref, v_ref, kf_ref, sub_ref,
                       dslope_ref, dq_ref, dkn_ref, dvn_ref, ck_hbm, cv_hbm, o_ref, do_ref,
                       acc_sc, s_sc, kbuf, vbuf, sem, *, blk, nq, post_scale, n_pages, layer, n_steps):
    step = pl.program_id(0)
    ahead = PAGE_SLOTS - 1
    slot = step % PAGE_SLOTS

    def page_copy(which, seq, to_slot, j):
        src, dst = (ck_hbm, kbuf) if which == 0 else (cv_hbm, vbuf)
        return pltpu.make_async_copy(src.at[layer, pt_ref[seq * n_pages + j]], dst.at[to_slot, j],
                                     sem.at[to_slot, which])

    def start_pages(seq, to_slot):
        for j in range(n_pages):
            page_copy(0, seq, to_slot, j).start()
        for j in range(n_pages):
            page_copy(1, seq, to_slot, j).start()

    def wait_pages(which):
        for j in range(n_pages):
            page_copy(which, step, slot, j).wait()

    @pl.when(step == 0)
    def _():
        for first in range(min(ahead, n_steps)):
            start_pages(first, first)

    @pl.when(step + ahead < n_steps)
    def _():
        start_pages(step + ahead, (step + ahead) % PAGE_SLOTS)

    wait_pages(0)
    k_pages = [kbuf.at[slot, j] for j in range(n_pages)]
    v_pages = [vbuf.at[slot, j] for j in range(n_pages)]

    finish = _paged_attn_begin(dslope_ref[...], dq_ref[0], dkn_ref[0], dvn_ref[0], k_pages, s_sc,
                               n_pages)

    def decode_finish():
        wait_pages(1)
        do_ref[0] = finish(par_ref[0], sub_ref[...], v_pages, post_scale)

    _attn_shift_body((step // nq) % H_A, step % nq, par_ref, slope_ref, nkeep_ref, q_ref, qf_ref,
                     k_ref, v_ref, kf_ref, sub_ref, o_ref, acc_sc, blk, post_scale,
                     between_diagonal=decode_finish)


def _attn_prompt_specs(nq, seq, blk, index):
    smem = pl.BlockSpec(memory_space=pltpu.SMEM)
    return [
        smem, smem, smem,
        pl.BlockSpec((blk, DV_A), index(lambda b, h, i: (b * nq + i, h))),
        pl.BlockSpec((None, blk, DV_A), index(lambda b, h, i: (h, 0, 0))),
        pl.BlockSpec((seq, DV_A), index(lambda b, h, i: (b, h))),
        pl.BlockSpec((seq, DV_A), index(lambda b, h, i: (b, h))),
        pl.BlockSpec((blk, DV_A), index(lambda b, h, i: (0, 0))),
        pl.BlockSpec((1, DV_A), index(lambda b, h, i: (0, 0))),
    ]


def _attn_prompt(par, slopes, nkeep, q, qf, k, v, kf, sub, nb, seq, blk, post_scale, shifted):
    t = nb * seq
    nq = seq // blk
    if shifted:
        body = functools.partial(_attn_shift_kernel, blk=blk, post_scale=post_scale)
        scratch = [pltpu.VMEM((2, blk, 2 * DV_A), F32)]
        name = "attn_prompt_shift"
    else:
        body = functools.partial(_attn_kernel, blk=blk, post_scale=post_scale)
        scratch = [pltpu.VMEM((2, blk, 1), F32), pltpu.VMEM((2, blk, 1), F32),
                   pltpu.VMEM((2, blk, DV_A), F32)]
        name = "attn_prompt_online"
    return pl.pallas_call(
        body,
        grid=(nb, H_A, nq),
        in_specs=_attn_prompt_specs(nq, seq, blk, lambda f: f),
        out_specs=pl.BlockSpec((blk, DV_A), lambda b, h, i: (b * nq + i, h)),
        out_shape=jax.ShapeDtypeStruct((t, W_A), F32),
        scratch_shapes=scratch,
        name=name,
        compiler_params=pltpu.CompilerParams(
            dimension_semantics=("parallel", "parallel", "arbitrary"),
            vmem_limit_bytes=VMEM_LIMIT),
    )(par, slopes, nkeep, q, qf, k, v, kf, sub)


def _attn_fused(pt_flat, par, slopes, nkeep, q, qf, k, v, kf, sub, slope_rows, dq, dkn, dvn, ck, cv,
                layer, n_pages, nb, seq, blk, post_scale):
    t = nb * seq
    nq = seq // blk
    db = dq.shape[0]
    rows = PAGE_SIZE * H_A

    def with_pt(f):
        return lambda n, pt: f(n // (H_A * nq), (n // nq) % H_A, n % nq)

    def seq_index(n, pt):
        return (n, 0, 0)

    any_spec = pl.BlockSpec(memory_space=pl.ANY)
    in_specs = _attn_prompt_specs(nq, seq, blk, with_pt)
    in_specs += [pl.BlockSpec((ATT_ROWS, 1), lambda n, pt: (0, 0)),
                 pl.BlockSpec((1, 1, QK_A), seq_index),
                 pl.BlockSpec((1, 1, QK_A), seq_index),
                 pl.BlockSpec((1, 1, W_A), seq_index),
                 any_spec, any_spec]
    grid_spec = pltpu.PrefetchScalarGridSpec(
        num_scalar_prefetch=1,
        grid=(nb * H_A * nq,),
        in_specs=in_specs,
        out_specs=[pl.BlockSpec((blk, DV_A), with_pt(lambda b, h, i: (b * nq + i, h))),
                   pl.BlockSpec((1, H_A, DV_A), seq_index)],
        scratch_shapes=[pltpu.VMEM((2, blk, 2 * DV_A), F32),
                        pltpu.VMEM((ATT_ROWS, n_pages * rows), F32),
                        pltpu.VMEM((PAGE_SLOTS, n_pages, rows, DV_A), F32),
                        pltpu.VMEM((PAGE_SLOTS, n_pages, rows, DV_A), F32),
                        pltpu.SemaphoreType.DMA((PAGE_SLOTS, 2))],
    )
    return pl.pallas_call(
        functools.partial(_attn_fused_kernel, blk=blk, nq=nq, post_scale=post_scale, n_pages=n_pages,
                          layer=layer, n_steps=nb * H_A * nq),
        grid_spec=grid_spec,
        out_shape=[jax.ShapeDtypeStruct((t, W_A), F32), jax.ShapeDtypeStruct((db, H_A, DV_A), F32)],
        name="attn_fused",
        compiler_params=pltpu.CompilerParams(
            dimension_semantics=("arbitrary",),
            vmem_limit_bytes=VMEM_LIMIT),
    )(pt_flat, par, slopes, nkeep, q, qf, k, v, kf, sub, slope_rows,
      dq.reshape(db, 1, QK_A), dkn.reshape(db, 1, QK_A), dvn.reshape(db, 1, W_A), ck, cv)


def _block_diag_rows(m):
    head = lax.broadcasted_iota(jnp.int32, m.shape, 1) // DK_C
    parts = [jnp.where(head == hp, m, 0.0).astype(BF16) for hp in range(H_C)]
    return jnp.concatenate(parts, axis=0)


def _mix_kernel(x_ref, oa_ref, u_ref, g_ref, kk_ref, vc_ref, qh_ref, sg_ref,
                ltri_ref, hh_ref, pw_ref, ps_ref, hg_ref, wo_ref,
                xo_ref, pool_ref, hst_ref,
                ubuf, kkbuf, gbuf, vcbuf, st_sc, inter_sc, xab_sc, *, tc):
    si = pl.program_id(1)
    ns = pl.num_programs(1)
    pad = HGRN_SUB
    nchunk = tc // HGRN_CHUNK

    @pl.when(si == 0)
    def _():
        st_sc[...] = jnp.zeros(st_sc.shape, F32)
        z = jnp.zeros((pad, W_B), F32)
        ubuf[0:pad, :] = z
        kkbuf[0:pad, :] = z
        gbuf[0:pad, :] = z
        vcbuf[0:pad, :] = z

    u = u_ref[...]
    ubuf[pad:pad + tc, :] = u
    acc = ubuf[...]
    wsum = {}
    for k, win in enumerate(POOL_WINDOWS):
        acc = acc + pltpu.roll(acc, 2 ** k, 0)
        wsum[win] = acc[pad:, :]
    pos = (si * tc + lax.broadcasted_iota(jnp.int32, (tc, 1), 0)).astype(F32)
    lane_b = lax.broadcasted_iota(jnp.int32, (tc, W_B), 1)
    pooled = None
    for gi, win in reversed(list(enumerate(POOL_WINDOWS))):
        term = wsum[win] * (1.0 / jnp.minimum(pos + 1.0, float(win)))
        pooled = term if pooled is None else jnp.where(lane_b < (gi + 1) * C_B, term, pooled)
    pooled = pooled - u
    ob = _dot(pooled.astype(BF16), pw_ref[...]) * ps_ref[...]
    ubuf[0:pad, :] = ubuf[tc:tc + pad, :]

    x_ab = (x_ref[...]
            + _dot((oa_ref[...] * sg_ref[:, 0:W_A]).astype(BF16), wo_ref[0:W_A, :])
            + _dot((ob * sg_ref[:, W_A:W_A + W_B]).astype(BF16), wo_ref[W_A:W_A + W_B, :]))
    xab_sc[...] = x_ab

    g = g_ref[...]
    kk = kk_ref[...]
    vc = vc_ref[...]
    qh = qh_ref[...]
    ltri = ltri_ref[...]
    ghi, gmid, glo = _split3(g)
    gc = _dot(ltri, ghi) + _dot(ltri, gmid) + _dot(ltri, glo)
    gc3 = gc.reshape(nchunk, HGRN_CHUNK, W_C)

    def chunk_row(idx):
        r = jnp.broadcast_to(gc3[:, idx:idx + 1, :], (nchunk, HGRN_CHUNK, W_C))
        return r.reshape(tc, W_C)

    rc = lax.broadcasted_iota(jnp.int32, (tc, 1), 0) % HGRN_CHUNK
    hh = hh_ref[...]
    t_idx = lax.broadcasted_iota(jnp.int32, (HGRN_CHUNK, H_C * HGRN_CHUNK), 0)
    s_idx = lax.broadcasted_iota(jnp.int32, (HGRN_CHUNK, H_C * HGRN_CHUNK), 1) % HGRN_CHUNK
    chunks = [slice(c * HGRN_CHUNK, (c + 1) * HGRN_CHUNK) for c in range(nchunk)]

    def hgrn_projection(o):
        ss = _dot((o * o).astype(BF16), hh)
        ocn = o * lax.rsqrt(ss * (1.0 / DV_C) + EPS) * hg_ref[...]
        return _dot((ocn * sg_ref[:, W_A + W_B:]).astype(BF16), wo_ref[W_A + W_B:, :])

    rmid = chunk_row(HGRN_CHUNK // 2 - 1)
    eg = jnp.exp(gc)
    qe = qh * eg
    kdec = kk * jnp.exp(chunk_row(HGRN_CHUNK - 1) - gc)
    qt = qh * jnp.exp(gc - rmid)
    kt = kk * jnp.exp(rmid - gc)
    bdmask = (lax.broadcasted_iota(jnp.int32, (W_C, W_C), 0) // DK_C
              == lax.broadcasted_iota(jnp.int32, (W_C, W_C), 1) // DK_C)
    outs = []
    for c, sl in enumerate(chunks):
        a = _dot_nt(qt[sl].astype(BF16), _block_diag_rows(kt[sl]))
        a = jnp.where(t_idx >= s_idx, a, 0.0)
        st = st_sc[...]
        inter = _dot_nt(qe[sl].astype(BF16), st.astype(BF16))
        inter_sc[sl, :] = inter
        outs.append(_dot(a.astype(BF16), _block_diag_rows(vc[sl])) + inter)
        last = c * HGRN_CHUNK + HGRN_CHUNK - 1
        upd = _dot_tn(vc[sl].astype(BF16), kdec[sl].astype(BF16))
        st_sc[...] = st * eg[last:last + 1, :] + jnp.where(bdmask, upd, 0.0)
    xo_ref[...] = x_ab + hgrn_projection(jnp.concatenate(outs, axis=0))

    safe = jnp.max(jnp.abs(gc - rmid)) <= HGRN_SAFE_EXP

    @pl.when(jnp.logical_not(safe))
    def _():
        upper = rc >= 32
        r31 = chunk_row(31)
        qt1 = jnp.where(upper, qh * jnp.exp(jnp.minimum(gc - r31, 0.0)), 0.0)
        kt1 = jnp.where(upper, 0.0, kk * jnp.exp(jnp.minimum(r31 - gc, 0.0)))
        ref2 = jnp.where(upper, chunk_row(47), chunk_row(15))
        odd = ((rc // HGRN_SUB) % 2) == 1
        qt2 = jnp.where(odd, qh * jnp.exp(jnp.minimum(gc - ref2, 0.0)), 0.0)
        kt2 = jnp.where(odd, 0.0, kk * jnp.exp(jnp.minimum(ref2 - gc, 0.0)))
        tb = t_idx // HGRN_SUB
        mask2 = ((tb % 2) == 1) & (s_idx // HGRN_SUB == tb - 1)

        kkbuf[pad:pad + tc, :] = kk
        gbuf[pad:pad + tc, :] = gc
        vcbuf[pad:pad + tc, :] = vc
        r16 = rc % HGRN_SUB
        od = jnp.zeros((tc, W_C), F32)
        for d in range(HGRN_SUB):
            kks = kkbuf[pad - d:pad - d + tc, :]
            gs = gbuf[pad - d:pad - d + tc, :]
            vcs = vcbuf[pad - d:pad - d + tc, :]
            xd = jnp.where(r16 >= d, qh * kks * jnp.exp(jnp.minimum(gc - gs, 0.0)), 0.0)
            od = od + _dot(xd.astype(BF16), hh) * vcs
        redo = []
        for sl in chunks:
            a1 = _dot_nt(qt1[sl].astype(BF16), _block_diag_rows(kt1[sl]))
            a2 = _dot_nt(qt2[sl].astype(BF16), _block_diag_rows(kt2[sl]))
            a_off = a1 + jnp.where(mask2, a2, 0.0)
            redo.append(_dot(a_off.astype(BF16), _block_diag_rows(vc[sl])) + od[sl] + inter_sc[sl, :])
        xo_ref[...] = xab_sc[...] + hgrn_projection(jnp.concatenate(redo, axis=0))

    @pl.when(si == ns - 1)
    def _():
        pool_ref[0] = u[tc - POOL_BUF:, :]
        hst_ref[0] = st_sc[...]


def _mix_prompt(x, oa, u, g, kk, vc, qh, sg, ltri, hh, pw, ps, hg, wo, layer, nb, seq, tc):
    t = nb * seq
    ns = seq // tc
    row = lambda b, s: (b * ns + s, 0)
    fix = lambda b, s: (0, 0)
    lay = lambda b, s: (layer, 0, 0)
    in_specs = [pl.BlockSpec((tc, D_MODEL), row), pl.BlockSpec((tc, W_A), row)]
    in_specs += [pl.BlockSpec((tc, W_C), row) for _ in range(5)]
    in_specs += [
        pl.BlockSpec((tc, D_MODEL), row),
        pl.BlockSpec((tc, tc), fix),
        pl.BlockSpec((W_C, W_C), fix),
        pl.BlockSpec((None, W_B, W_B), lay),
        pl.BlockSpec((1, W_B), fix),
        pl.BlockSpec((1, W_C), fix),
        pl.BlockSpec((None, D_MODEL, D_MODEL), lay),
    ]
    return pl.pallas_call(
        functools.partial(_mix_kernel, tc=tc),
        grid=(nb, ns),
        in_specs=in_specs,
        out_specs=[
            pl.BlockSpec((tc, D_MODEL), row),
            pl.BlockSpec((1, POOL_BUF, W_B), lambda b, s: (b, 0, 0)),
            pl.BlockSpec((1, W_C, W_C), lambda b, s: (b, 0, 0)),
        ],
        out_shape=[
            jax.ShapeDtypeStruct((t, D_MODEL), F32),
            jax.ShapeDtypeStruct((nb, POOL_BUF, W_B), F32),
            jax.ShapeDtypeStruct((nb, W_C, W_C), F32),
        ],
        scratch_shapes=[
            pltpu.VMEM((tc + HGRN_SUB, W_B), F32),
            pltpu.VMEM((tc + HGRN_SUB, W_C), F32),
            pltpu.VMEM((tc + HGRN_SUB, W_C), F32),
            pltpu.VMEM((tc + HGRN_SUB, W_C), F32),
            pltpu.VMEM((W_C, W_C), F32),
            pltpu.VMEM((tc, W_C), F32),
            pltpu.VMEM((tc, D_MODEL), F32),
        ],
        name="mix_prompt",
        compiler_params=pltpu.CompilerParams(
            dimension_semantics=("parallel", "arbitrary"), vmem_limit_bytes=VMEM_LIMIT),
    )(x, oa, u, g, kk, vc, qh, sg, ltri, hh, pw, ps, hg, wo)


ATT_ROWS = 2 * H_A
PAGE_SLOTS = 3


def _paged_attn_begin(slope, q_row, kn_row, vn_row, k_refs, s_sc, n_pages):
    rows = PAGE_SIZE * H_A
    past = n_pages * PAGE_SIZE

    r = lax.broadcasted_iota(jnp.int32, (ATT_ROWS, DV_A), 0)
    lane = lax.broadcasted_iota(jnp.int32, (ATT_ROWS, DV_A), 1)

    def head_rows(row):
        out = jnp.zeros((ATT_ROWS, DV_A), F32)
        for h in range(H_A):
            piece = jnp.broadcast_to(row[:, h * DV_A:(h + 1) * DV_A], (ATT_ROWS, DV_A))
            out = jnp.where(r % H_A == h, piece, out)
        return out

    q = jnp.where(lane // DK_A == r // (ATT_ROWS // 2), head_rows(q_row), 0.0)
    kn = head_rows(kn_row)
    vn = head_rows(vn_row)
    qb = q.astype(BF16)

    col = lax.broadcasted_iota(jnp.int32, (ATT_ROWS, rows), 1)
    rr = lax.broadcasted_iota(jnp.int32, (ATT_ROWS, rows), 0)
    valid = (col % H_A) == (rr % H_A)
    tok = col // H_A
    for j in range(n_pages):
        s = _dot_nt(qb, k_refs[j][...].astype(BF16))
        dist = (past - j * PAGE_SIZE - tok).astype(F32)
        s_sc[:, j * rows:(j + 1) * rows] = jnp.where(valid, s - slope * dist, NEG)
    s_self = jnp.sum(q * kn, axis=-1, keepdims=True)

    def finish(lam, sub, v_refs, post_scale):
        s_all = s_sc[...]
        m = jnp.maximum(jnp.max(s_all, axis=-1, keepdims=True), s_self)
        p_self = jnp.exp(s_self - m)
        l = p_self
        acc = p_self * vn
        for j in range(n_pages):
            p = jnp.exp(s_all[:, j * rows:(j + 1) * rows] - m)
            l = l + jnp.sum(p, axis=-1, keepdims=True)
            acc = acc + _dot(p.astype(BF16), v_refs[j][...].astype(BF16))
        o_maps = acc / l
        half = ATT_ROWS // 2
        o = o_maps[0:half] - lam * o_maps[half:ATT_ROWS]
        on = o * lax.rsqrt(jnp.mean(o * o, axis=-1, keepdims=True) + EPS) * sub
        return (on * post_scale)[0:H_A]

    return finish


def _paged_attn_kernel(pt_ref, lam_ref, slope_ref, q_ref, kn_ref, vn_ref, sub_ref, *rest,
                       n_pages, post_scale):
    del pt_ref
    k_refs = rest[:n_pages]
    v_refs = rest[n_pages:2 * n_pages]
    o_ref = rest[2 * n_pages]
    s_sc = rest[2 * n_pages + 1]
    finish = _paged_attn_begin(slope_ref[...], q_ref[0], kn_ref[0], vn_ref[0], k_refs, s_sc, n_pages)
    o_ref[0] = finish(lam_ref[0], sub_ref[...], v_refs, post_scale)


def _attn_paged(pt_flat, lam, slope_rows, q, kn, vn, sub, ck, cv, layer, n_pages, post_scale):
    nb = q.shape[0]
    rows = PAGE_SIZE * H_A
    smem = pl.BlockSpec(memory_space=pltpu.SMEM)
    row3 = lambda b, pt: (b, 0, 0)

    def page_spec(j):
        return pl.BlockSpec((None, None, rows, DV_A),
                            lambda b, pt, j=j: (layer, pt[b * n_pages + j], 0, 0))

    in_specs = [smem,
                pl.BlockSpec((ATT_ROWS, 1), lambda b, pt: (0, 0)),
                pl.BlockSpec((1, 1, QK_A), row3),
                pl.BlockSpec((1, 1, QK_A), row3),
                pl.BlockSpec((1, 1, W_A), row3),
                pl.BlockSpec((1, DV_A), lambda b, pt: (0, 0))]
    in_specs += [page_spec(j) for j in range(n_pages)] * 2
    grid_spec = pltpu.PrefetchScalarGridSpec(
        num_scalar_prefetch=1,
        grid=(nb,),
        in_specs=in_specs,
        out_specs=pl.BlockSpec((1, H_A, DV_A), row3),
        scratch_shapes=[pltpu.VMEM((ATT_ROWS, n_pages * rows), F32)],
    )
    return pl.pallas_call(
        functools.partial(_paged_attn_kernel, n_pages=n_pages, post_scale=post_scale),
        grid_spec=grid_spec,
        out_shape=jax.ShapeDtypeStruct((nb, H_A, DV_A), F32),
        name="attn_paged",
        compiler_params=pltpu.CompilerParams(
            dimension_semantics=("arbitrary",), vmem_limit_bytes=VMEM_LIMIT),
    )(pt_flat, lam, slope_rows, q.reshape(nb, 1, QK_A), kn.reshape(nb, 1, QK_A),
      vn.reshape(nb, 1, W_A), sub, *([ck] * n_pages), *([cv] * n_pages))


def _dec_mix_kernel(x_ref, oa_ref, u_ref, gt_ref, kkt_ref, vct_ref, qht_ref, sg_ref, sp_ref, sh_ref,
                    pall_ref, hall_ref, pw_ref, ps_ref, hgc_ref, wo_ref,
                    xo_ref, po_ref, ho_ref, ot_sc, *, past):
    del pall_ref, hall_ref
    h = pl.program_id(0)
    nb = x_ref.shape[0]

    @pl.when(h == 0)
    def _():
        u = u_ref[...]
        lane_b = lax.broadcasted_iota(jnp.int32, (nb, W_B), 1)
        acc = u
        wsum = {}
        for j in range(1, max(POOL_WINDOWS)):
            acc = acc + sp_ref[POOL_BUF - j]
            if j + 1 in POOL_WINDOWS:
                wsum[j + 1] = acc
        pooled = None
        for gi, win in reversed(list(enumerate(POOL_WINDOWS))):
            term = wsum[win] * (1.0 / min(past + 1, win))
            pooled = term if pooled is None else jnp.where(lane_b < (gi + 1) * C_B, term, pooled)
        pooled = pooled - u
        for j in range(POOL_BUF - 1):
            po_ref[j] = sp_ref[j + 1]
        po_ref[POOL_BUF - 1] = u
        ob = _dot(pooled.astype(BF16), pw_ref[...]) * ps_ref[...]
        xo_ref[...] = (x_ref[...]
                       + _dot((oa_ref[...] * sg_ref[:, 0:W_A]).astype(BF16), wo_ref[0:W_A, :])
                       + _dot((ob * sg_ref[:, W_A:W_A + W_B]).astype(BF16), wo_ref[W_A:W_A + W_B, :]))

    base = pl.multiple_of(h * DK_C, DK_C)
    vt = vct_ref[pl.ds(base, DV_C), :]

    def per_d(d, acc):
        r = base + d
        s_new = jnp.exp(gt_ref[pl.ds(r, 1), :]) * sh_ref[d] + kkt_ref[pl.ds(r, 1), :] * vt
        ho_ref[d] = s_new
        return acc + qht_ref[pl.ds(r, 1), :] * s_new

    ot_sc[pl.ds(base, DV_C), :] = lax.fori_loop(0, DK_C, per_d, jnp.zeros((DV_C, nb), F32))

    @pl.when(h == pl.num_programs(0) - 1)
    def _():
        ot = ot_sc[...].reshape(H_C, DV_C, nb)
        otn = ot * lax.rsqrt(jnp.mean(ot * ot, axis=1, keepdims=True) + EPS)
        mix_t = otn.reshape(W_C, nb) * hgc_ref[...] * sg_ref[:, W_A + W_B:].T
        xo_ref[...] += _dot_tn(mix_t.astype(BF16), wo_ref[W_A + W_B:, :])


def _mix_decode(x, oa, u, gt, kkt, vct, qht, sg, sp, sh, pool_all, hgrn_all, pw, ps, hgc, wo,
                layer, past):
    nb = x.shape[0]
    fix = lambda h: (0, 0)
    pool_spec = pl.BlockSpec((None, POOL_BUF, nb, W_B), lambda h: (layer, 0, 0, 0))
    state_spec = pl.BlockSpec((None, None, DK_C, DV_C, nb), lambda h: (layer, h, 0, 0, 0))
    any_spec = pl.BlockSpec(memory_space=pl.ANY)
    in_specs = [
        pl.BlockSpec((nb, D_MODEL), fix),
        pl.BlockSpec((nb, W_A), fix),
        pl.BlockSpec((nb, W_B), fix),
        pl.BlockSpec((W_C, nb), fix),
        pl.BlockSpec((W_C, nb), fix),
        pl.BlockSpec((W_C, nb), fix),
        pl.BlockSpec((W_C, nb), fix),
        pl.BlockSpec((nb, D_MODEL), fix),
        pool_spec,
        state_spec,
        any_spec,
        any_spec,
        pl.BlockSpec((None, W_B, W_B), lambda h: (layer, 0, 0)),
        pl.BlockSpec((1, W_B), fix),
        pl.BlockSpec((None, W_C, 1), lambda h: (layer, 0, 0)),
        pl.BlockSpec((None, D_MODEL, D_MODEL), lambda h: (layer, 0, 0)),
    ]
    return pl.pallas_call(
        functools.partial(_dec_mix_kernel, past=past),
        grid=(H_C,),
        in_specs=in_specs,
        out_specs=[pl.BlockSpec((nb, D_MODEL), fix), pool_spec, state_spec],
        out_shape=[
            jax.ShapeDtypeStruct((nb, D_MODEL), F32),
            jax.ShapeDtypeStruct(pool_all.shape, F32),
            jax.ShapeDtypeStruct(hgrn_all.shape, F32),
        ],
        scratch_shapes=[pltpu.VMEM((W_C, nb), F32)],
        input_output_aliases={10: 1, 11: 2},
        name="mix_decode",
        compiler_params=pltpu.CompilerParams(
            dimension_semantics=("arbitrary",), vmem_limit_bytes=VMEM_LIMIT),
    )(x, oa, u, gt, kkt, vct, qht, sg, sp, sh, pool_all, hgrn_all, pw, ps, hgc, wo)


def _block_ones(n, blk):
    idx = np.arange(n) // blk
    return jnp.asarray((idx[:, None] == idx[None, :]).astype(np.float32), dtype=BF16)


def _chunk_lower_tri(n, blk):
    idx = np.arange(n)
    same = (idx[:, None] // blk) == (idx[None, :] // blk)
    return jnp.asarray((same & (idx[:, None] >= idx[None, :])).astype(np.float32), dtype=BF16)


def kernel(x_prompt, x_sample, cache_k, cache_v, state_pool, state_hgrn, page_table,
           ln_gain, w_in, qn_gain, kn_gain, lam_q1, lam_k1, lam_q2, lam_k2, subln_gain,
           pool_w, pool_scale, hgrn_lb, hgrn_norm_gain, w_out):
    nb, seq, _ = x_prompt.shape
    t = nb * seq
    db = x_sample.shape[0]
    n_pages = page_table.shape[1]
    past = n_pages * PAGE_SIZE
    n_phys = cache_k.shape[1]

    tm = min(512, t)
    blk = min(512, seq)
    tc = min(256, seq)

    w_in_b = w_in.astype(BF16)
    w_out_b = w_out.astype(BF16)
    sm = jax.nn.softmax(hgrn_lb.astype(F32), axis=0)
    lb_all = jnp.cumsum(sm, axis=0) - sm[0]
    lam_init = [0.8 - 0.6 * math.exp(-0.3 * l) for l in range(DEPTH)]
    lam_all = (jnp.exp(jnp.sum(lam_q1.astype(F32) * lam_k1.astype(F32), axis=-1))
               - jnp.exp(jnp.sum(lam_q2.astype(F32) * lam_k2.astype(F32), axis=-1))
               + jnp.asarray(lam_init, F32))
    slopes_np = np.asarray([2.0 ** (-8.0 * (h + 1) / H_A) for h in range(H_A)], np.float32)
    slopes = jnp.asarray(slopes_np)
    slope_rows = jnp.asarray(slopes_np[np.arange(ATT_ROWS) % H_A].reshape(ATT_ROWS, 1))
    score_bound = (1.0125 * DK_A ** 0.5) * jnp.max(
        jnp.abs(qn_gain.astype(F32) * kn_gain.astype(F32)), axis=-1)
    sb_hi = score_bound.astype(BF16).astype(F32)
    sb_mid = (score_bound - sb_hi).astype(BF16).astype(F32)
    sb_lo = (score_bound - sb_hi - sb_mid).astype(BF16).astype(F32)
    attn_par = jnp.stack([lam_all, sb_hi, sb_mid, sb_lo], axis=1)
    kfeat = _attn_k_features(blk)
    qfeat = [_attn_q_features(blk, slopes_np, attn_par[l, 1:]) for l in range(DEPTH)]
    nkeep = _attn_blocks_kept(blk, seq // blk, slopes_np)
    fuse_decode = nb * H_A * (seq // blk) == db
    qg = jnp.tile(qn_gain.astype(F32), (1, QK_A // DK_A))
    kg = jnp.tile(kn_gain.astype(F32), (1, QK_A // DK_A))
    hg_tiled = jnp.tile(hgrn_norm_gain.astype(F32), (1, H_C))
    eye_g = jnp.eye(G_B, dtype=F32)
    pw_bd = jnp.einsum('lgcd,gh->lgchd', pool_w.astype(F32), eye_g).reshape(DEPTH, W_B, W_B).astype(BF16)
    hh = _block_ones(W_C, DV_C)
    ltri = _chunk_lower_tri(tc, HGRN_CHUNK)
    ck = cache_k.reshape(DEPTH, n_phys, PAGE_SIZE * H_A, 2 * DK_A)
    cv = cache_v.reshape(DEPTH, n_phys, PAGE_SIZE * H_A, DV_A)
    pt_flat = page_table.reshape(-1).astype(jnp.int32)

    xp = x_prompt.reshape(t, D_MODEL)
    xs = x_sample.reshape(db, D_MODEL)
    k_all = jnp.zeros((DEPTH, t * H_A, 2 * DK_A), F32)
    v_all = jnp.zeros((DEPTH, t * H_A, DV_A), F32)
    pools, states = [], []
    dks, dvs = [], []
    sp_in = jnp.transpose(state_pool, (0, 2, 1, 3))
    sh_in = jnp.transpose(state_hgrn, (0, 2, 3, 4, 1))
    dpool_all = jnp.zeros(sp_in.shape, F32)
    dstate_all = jnp.zeros(sh_in.shape, F32)
    hg_col = jnp.tile(hgrn_norm_gain.astype(F32), (1, H_C)).reshape(DEPTH, W_C, 1)
    for l in range(DEPTH):
        lng = ln_gain[l].reshape(1, D_MODEL).astype(F32)
        lb = lb_all[l].reshape(1, W_C)
        sub = subln_gain[l].reshape(1, DV_A).astype(F32)
        ps = pool_scale[l].reshape(1, W_B).astype(F32)
        post = 1.0 - lam_init[l]

        q, k_all, kb, v_all, vb, u, g, kk, vc, qh, sg = _inproj_prompt(
            xp, lng, w_in_b, qg[l:l + 1], kg[l:l + 1], lb, hh, k_all, v_all, l, tm)
        dq, dk, dv, du, dgt, dkkt, dvct, dqht, dsg = _inproj_decode(
            xs, lng, w_in_b, qg[l:l + 1], kg[l:l + 1], lb, hh, l)

        def attend(par, qf, q, kb, vb, sub, dq, dk, dv, shifted, layer=l, post=post):
            if shifted and fuse_decode:
                return _attn_fused(pt_flat, par, slopes, nkeep, q, qf, kb, vb, kfeat, sub, slope_rows,
                                   dq, dk, dv, ck, cv, layer, n_pages, nb, seq, blk, post)
            oa = _attn_prompt(par, slopes, nkeep, q, qf, kb, vb, kfeat, sub, nb, seq, blk, post, shifted)
            doa = _attn_paged(pt_flat, par[0:1], slope_rows, dq, dk, dv, sub, ck, cv, layer,
                              n_pages, post)
            return oa, doa

        oa, doa = lax.cond(
            score_bound[l] <= ATTN_SAFE_SHIFT,
            functools.partial(attend, shifted=True), functools.partial(attend, shifted=False),
            attn_par[l], qfeat[l], q, kb, vb, sub, dq, dk, dv)
        xp, pool_new, st = _mix_prompt(xp, oa, u, g, kk, vc, qh, sg, ltri, hh, pw_bd, ps,
                                       hg_tiled[l:l + 1], w_out_b, l, nb, seq, tc)
        pools.append(pool_new)
        states.append(st)
        xs, dpool_all, dstate_all = _mix_decode(
            xs, doa.reshape(db, W_A), du, dgt, dkkt, dvct, dqht, dsg, sp_in, sh_in,
            dpool_all, dstate_all, pw_bd, ps, hg_col, w_out_b, l, past)
        dks.append(dk)
        dvs.append(dv)

    y_prompt = xp.reshape(nb, seq, D_MODEL)
    y_sample = xs.reshape(db, 1, D_MODEL)
    k_prompt = k_all.reshape(DEPTH, nb, seq, H_A, 2 * DK_A)
    v_prompt = v_all.reshape(DEPTH, nb, seq, H_A, DV_A)
    k_sample = jnp.stack(dks).reshape(DEPTH, db, 1, H_A, 2 * DK_A)
    v_sample = jnp.stack(dvs).reshape(DEPTH, db, 1, H_A, DV_A)
    pool_prompt = jnp.stack(pools)
    pool_sample = jnp.transpose(dpool_all, (0, 2, 1, 3))
    st_all = jnp.stack(states).reshape(DEPTH, nb, H_C, DV_C, H_C, DK_C)
    hgrn_prompt = jnp.stack([st_all[:, :, h, :, h, :] for h in range(H_C)], axis=2).swapaxes(-1, -2)
    hgrn_sample = jnp.transpose(dstate_all, (0, 4, 1, 2, 3))
    return (y_prompt, y_sample, k_prompt, v_prompt, k_sample, v_sample,
            pool_prompt, pool_sample, hgrn_prompt, hgrn_sample)
```
